```python
import math
import jax, jax.numpy as jnp
from jax import lax
import numpy as np

D_MODEL = 2048
BATCH = 4
SEQ = 8192
DEPTH = 2
DEC_BATCH = 1
DEC_SEQ = 16384
PAST_LEN = 128

N_MEM = 256
EPS = 1e-6
NEG_INF = -1e30

A_HEADS = 12
A_HEAD_DIM = 128
A_WIDTH = A_HEADS * A_HEAD_DIM
A_PATTERNS = ((128, 1), (512, 4), (2048, 16))
B_WIDTH = 512
B_GROUP = 16
B_GROUPS = B_WIDTH // B_GROUP
B_STATE = 64
C_HEADS = 8
C_KEY_DIM = 128
C_VAL_DIM = 128
C_WIDTH = C_HEADS * C_KEY_DIM
C_CHUNK = 16
D_WIDTH = 1024
D_BLOCKS = 8
D_BLOCK_DIM = D_WIDTH // D_BLOCKS
D_CONV = 4
LRU_C = 8.0
X_HEADS = 4
X_HEAD_DIM = 128
X_WIDTH = X_HEADS * X_HEAD_DIM

N_EVEN = (DEPTH + 1) // 2
N_ODD = DEPTH // 2
EVEN_IN = 4 * A_WIDTH + 2 * B_WIDTH
EVEN_MIX = A_WIDTH + B_WIDTH
ODD_IN = 5 * C_WIDTH + 2 * D_WIDTH
ODD_MIX = C_WIDTH + D_WIDTH

kernel_name = 'hybrid_bidir_longnet_s5_hgrn2_rglru'

F32 = jnp.float32


def _rms_norm(x, g):
    xf = x.astype(F32)
    y = xf * lax.rsqrt(jnp.mean(xf * xf, axis=-1, keepdims=True) + EPS) * g.astype(F32)
    return y.astype(x.dtype)


def _alibi_slopes():
    return jnp.asarray(2.0 ** (-8.0 * np.arange(1, A_HEADS + 1) / A_HEADS), dtype=F32)


def _to_sub(t, dil):
    bt, s = t.shape[:2]
    rest = t.shape[2:]
    return t.reshape((bt, s // dil, dil) + rest).swapaxes(1, 2).reshape((bt * dil, s // dil) + rest)


def _from_sub(t, bt, dil):
    n, l = t.shape[:2]
    rest = t.shape[2:]
    return t.reshape((bt, dil, l) + rest).swapaxes(1, 2).reshape((bt, l * dil) + rest)


def _dilated_pattern(q, k, v, slopes, dil, radius):
    bt, s, h, hd = q.shape
    l_sub = s // dil
    blk = radius
    nb = -(-l_sub // blk)
    lp = nb * blk
    qs = jnp.pad(_to_sub(q, dil), ((0, 0), (0, lp - l_sub), (0, 0), (0, 0))).reshape(-1, nb, blk, h, hd)

    def windows(t):
        tp = jnp.pad(_to_sub(t, dil), ((0, 0), (blk, lp - l_sub + blk), (0, 0), (0, 0)))
        tp = tp.reshape(-1, nb + 2, blk, h, hd)
        return jnp.concatenate([tp[:, :-2], tp[:, 1:-1], tp[:, 2:]], axis=2)

    kw = windows(k)
    vw = windows(v)
    rel = jnp.arange(3 * blk)[None, :] - blk - jnp.arange(blk)[:, None]
    jidx = jnp.arange(nb)[:, None] * blk + jnp.arange(3 * blk)[None, :] - blk
    valid = (jnp.abs(rel) <= radius)[None] & ((jidx >= 0) & (jidx < l_sub))[:, None, :]
    bias = -slopes[:, None, None] * (dil * jnp.abs(rel)).astype(F32)[None]
    sc = jnp.einsum('nbqhd,nbkhd->nbhqk', qs, kw) + bias[None, None]
    sc = jnp.where(valid[None, :, None], sc, NEG_INF)
    m = jnp.max(sc, axis=-1)
    p = jnp.exp(sc - m[..., None])
    den = jnp.sum(p, axis=-1)
    acc = jnp.einsum('nbhqk,nbkhd->nbqhd', p, vw).reshape(-1, lp, h, hd)[:, :l_sub]
    m = m.transpose(0, 1, 3, 2).reshape(-1, lp, h)[:, :l_sub]
    den = den.transpose(0, 1, 3, 2).reshape(-1, lp, h)[:, :l_sub]
    return _from_sub(m, bt, dil), _from_sub(den, bt, dil), _from_sub(acc, bt, dil)


def _dilated_attention(q, k, v):
    q = q.astype(F32) * (A_HEAD_DIM ** -0.5)
    k = k.astype(F32)
    v = v.astype(F32)
    slopes = _alibi_slopes()
    stats = [_dilated_pattern(q, k, v, slopes, dil, win // (2 * dil)) for win, dil in A_PATTERNS]
    m_max = jnp.max(jnp.stack([st[0] for st in stats], axis=0), axis=0)
    num = 0.0
    den = 0.0
    for m, l, acc in stats:
        w = jnp.exp(m - m_max)
        num = num + w[..., None] * acc
        den = den + w * l
    return num / den[..., None]


def _complex_affine(e1, e2):
    a1r, a1i, b1r, b1i = e1
    a2r, a2i, b2r, b2i = e2
    return (a2r * a1r - a2i * a1i, a2r * a1i + a2i * a1r,
            a2r * b1r - a2i * b1i + b2r, a2r * b1i + a2i * b1r + b2i)


def _real_affine(e1, e2):
    a1, b1 = e1
    a2, b2 = e2
    return (a2 * a1, a2 * b1 + b2)


def _s5_bidirectional(u, lam_re, lam_im, log_step, b_re, b_im, c_re, c_im, d_skip):
    bt, s, _ = u.shape
    uf = u.astype(F32).reshape(bt, s, B_GROUPS, B_GROUP)
    y = d_skip.astype(F32).reshape(B_GROUPS, B_GROUP) * uf
    for di in range(2):
        lr = lam_re[di].astype(F32)
        li = lam_im[di].astype(F32)
        dt = jnp.exp(log_step[di].astype(F32))[:, None]
        e = jnp.exp(lr * dt)
        abar_re = e * jnp.cos(li * dt)
        abar_im = e * jnp.sin(li * dt)
        den = lr * lr + li * li
        nr = abar_re - 1.0
        ni = abar_im
        fr = (nr * lr + ni * li) / den
        fi = (ni * lr - nr * li) / den
        br = b_re[di].astype(F32)
        bi = b_im[di].astype(F32)
        bbar_re = fr[..., None] * br - fi[..., None] * bi
        bbar_im = fr[..., None] * bi + fi[..., None] * br
        bu_re = jnp.einsum('bsgc,gpc->bsgp', uf, bbar_re)
        bu_im = jnp.einsum('bsgc,gpc->bsgp', uf, bbar_im)
        a_re = jnp.broadcast_to(abar_re, (1, s, B_GROUPS, B_STATE))
        a_im = jnp.broadcast_to(abar_im, (1, s, B_GROUPS, B_STATE))
        _, _, xr, xi = lax.associative_scan(_complex_affine, (a_re, a_im, bu_re, bu_im),
                                            reverse=(di == 1), axis=1)
        y = y + jnp.einsum('bsgp,gcp->bsgc', xr, c_re[di].astype(F32)) \
              - jnp.einsum('bsgp,gcp->bsgc', xi, c_im[di].astype(F32))
    return y.reshape(bt, s, B_WIDTH)


def _hgrn2_chunk_scan(q, k, v, logf):
    bt, s, h, dk = q.shape
    dv = v.shape[-1]
    nc = s // C_CHUNK

    def to_chunks(t):
        return t.reshape(bt, nc, C_CHUNK, h, t.shape[-1]).transpose(1, 0, 3, 2, 4)

    tril = jnp.tril(jnp.ones((C_CHUNK, C_CHUNK), dtype=bool))

    def step(state, xs):
        qc, kc, vc, gc = xs
        b = jnp.cumsum(gc, axis=2)
        bl = b[:, :, -1:]
        qe = qc * jnp.exp(b)
        ke = kc * jnp.exp(-b)
        att = jnp.where(tril, jnp.einsum('bhtk,bhsk->bhts', qe, ke), 0.0)
        o = jnp.einsum('bhts,bhsv->bhtv', att, vc) + jnp.einsum('bhtk,bhkv->bhtv', qe, state)
        state = jnp.exp(bl[:, :, 0])[..., None] * state + \
            jnp.einsum('bhsk,bhsv->bhkv', kc * jnp.exp(bl - b), vc)
        return state, o

    state0 = jnp.zeros((bt, h, dk, dv), F32)
    _, o = lax.scan(step, state0, (to_chunks(q), to_chunks(k), to_chunks(v), to_chunks(logf)))
    return o.transpose(1, 0, 3, 2, 4).reshape(bt, s, h, dv)


def _hgrn2_bidirectional(q, f_fwd, f_bwd, i, lb):
    bt, s, _ = q.shape
    shp = (bt, s, C_HEADS, C_KEY_DIM)
    qf = jax.nn.silu(q.astype(F32)).reshape(shp)
    vf = i.astype(F32).reshape(bt, s, C_HEADS, C_VAL_DIM)
    lb = lb.astype(F32)
    out = 0.0
    for fx, rev in ((f_fwd, False), (f_bwd, True)):
        fx = fx.astype(F32)
        logf = jnp.log(lb + (1.0 - lb) * jax.nn.sigmoid(fx)).reshape(shp)
        kf = ((1.0 - lb) * jax.nn.sigmoid(-fx)).reshape(shp)
        qs, vs = qf, vf
        if rev:
            qs, kf, vs, logf = (jnp.flip(t, axis=1) for t in (qs, kf, vs, logf))
        o = _hgrn2_chunk_scan(qs, kf, vs, logf)
        if rev:
            o = jnp.flip(o, axis=1)
        out = out + o
    return out


def _rglru_bidirectional(x, conv_w, conv_b, wr, br, wi, bi, lam):
    bt, s, _ = x.shape
    xc = lax.conv_general_dilated(x.astype(F32), conv_w.astype(F32)[:, None, :], window_strides=(1,),
                                  padding=((1, 2),), dimension_numbers=('NWC', 'WIO', 'NWC'),
                                  feature_group_count=D_WIDTH) + conv_b.astype(F32)
    xb = xc.reshape(bt, s, D_BLOCKS, D_BLOCK_DIM)
    h = 0.0
    for di in range(2):
        r = jax.nn.sigmoid(jnp.einsum('bsnc,ncd->bsnd', xb, wr[di].astype(F32)).reshape(bt, s, D_WIDTH)
                           + br[di].astype(F32))
        ig = jax.nn.sigmoid(jnp.einsum('bsnc,ncd->bsnd', xb, wi[di].astype(F32)).reshape(bt, s, D_WIDTH)
                            + bi[di].astype(F32))
        log_a = -LRU_C * r * jax.nn.softplus(-lam[di].astype(F32))
        a = jnp.exp(log_a)
        bx = jnp.sqrt(-jnp.expm1(2.0 * log_a)) * (ig * xc)
        _, hs = lax.associative_scan(_real_affine, (a, bx), reverse=(di == 1), axis=1)
        h = h + hs
    return h


def _even_mixer(h, w_in, w_out, lam_re, lam_im, log_step, b_re, b_im, c_re, c_im, d_skip, glu_w, glu_b):
    bt, s, _ = h.shape
    z = h @ w_in
    qkv, g_a, u_b, g_b = jnp.split(z, [3 * A_WIDTH, 4 * A_WIDTH, 4 * A_WIDTH + B_WIDTH], axis=-1)
    q, k, v = [t.reshape(bt, s, A_HEADS, A_HEAD_DIM) for t in jnp.split(qkv, 3, axis=-1)]
    y_a = _dilated_attention(q, k, v).reshape(bt, s, A_WIDTH) * jax.nn.silu(g_a.astype(F32))
    y_s5 = jax.nn.gelu(_s5_bidirectional(u_b, lam_re, lam_im, log_step, b_re, b_im, c_re, c_im, d_skip))
    y_glu = y_s5 * jax.nn.sigmoid(y_s5 @ glu_w.astype(F32) + glu_b.astype(F32))
    y_b = y_glu * jax.nn.silu(g_b.astype(F32))
    y = jnp.concatenate([y_a, y_b], axis=-1).astype(h.dtype)
    return y @ w_out


def _odd_mixer(h, w_in, w_out, lb, onorm, conv_w, conv_b, wr, br, wi, bi, lam):
    bt, s, _ = h.shape
    z = h @ w_in
    q_c, f_fwd, f_bwd, i_c, g_c, x_d, g_d = jnp.split(
        z, [C_WIDTH, 2 * C_WIDTH, 3 * C_WIDTH, 4 * C_WIDTH, 5 * C_WIDTH, 5 * C_WIDTH + D_WIDTH], axis=-1)
    o_c = _hgrn2_bidirectional(q_c, f_fwd, f_bwd, i_c, lb)
    o_c = o_c * lax.rsqrt(jnp.mean(o_c * o_c, axis=-1, keepdims=True) + EPS)
    y_c = o_c.reshape(bt, s, C_WIDTH) * onorm.astype(F32) * jax.nn.silu(g_c.astype(F32))
    y_d = _rglru_bidirectional(x_d, conv_w, conv_b, wr, br, wi, bi, lam) * jax.nn.silu(g_d.astype(F32))
    y = jnp.concatenate([y_c, y_d], axis=-1).astype(h.dtype)
    return y @ w_out


def _cross_attention(h, mem, wq, wkv, wo, g_mem):
    bt, s, _ = h.shape
    m = _rms_norm(mem, g_mem)
    q = (h @ wq).reshape(bt, s, X_HEADS, X_HEAD_DIM).astype(F32)
    k, v = jnp.split(m @ wkv, 2, axis=-1)
    k = k.reshape(bt, N_MEM, X_HEADS, X_HEAD_DIM).astype(F32)
    v = v.reshape(bt, N_MEM, X_HEADS, X_HEAD_DIM).astype(F32)
    p = jax.nn.softmax(jnp.einsum('bshd,bmhd->bhsm', q, k) * (X_HEAD_DIM ** -0.5), axis=-1)
    o = jnp.einsum('bhsm,bmhd->bshd', p, v).reshape(bt, s, X_WIDTH).astype(h.dtype)
    return o @ wo


def _trunk(x, mem, p):
    lb_all = jnp.cumsum(jax.nn.softmax(p['hgrn_lb_logits'].astype(F32), axis=0), axis=0)
    lb_all = lb_all - lb_all[0:1]
    h = x
    for l in range(DEPTH):
        hn = _rms_norm(h, p['norm_mix_pre'][l])
        if l % 2 == 0:
            e = l // 2
            y = _even_mixer(hn, p['ev_w_in'][e], p['ev_w_out'][e], p['s5_lam_re'][e], p['s5_lam_im'][e],
                            p['s5_log_step'][e], p['s5_b_re'][e], p['s5_b_im'][e], p['s5_c_re'][e],
                            p['s5_c_im'][e], p['s5_d'][e], p['s5_glu_w'][e], p['s5_glu_b'][e])
        else:
            o = l // 2
            y = _odd_mixer(hn, p['od_w_in'][o], p['od_w_out'][o], lb_all[l], p['hgrn_onorm'][o],
                           p['lru_conv_w'][o], p['lru_conv_b'][o], p['lru_wr'][o], p['lru_br'][o],
                           p['lru_wi'][o], p['lru_bi'][o], p['lru_lambda'][o])
        h = h + _rms_norm(y, p['norm_mix_post'][l])
        hn = _rms_norm(h, p['norm_x_pre'][l])
        y = _cross_attention(hn, mem, p['x_wq'][l], p['x_wkv'][l], p['x_wo'][l], p['x_mem_norm'][l])
        h = h + _rms_norm(y, p['norm_x_post'][l])
    return h


def setup_inputs(seed: int = 0) -> dict:
    key = jax.random.key(seed)
    ks = jax.random.split(key, 40)

    def nrm(k, shape, scale):
        return jax.random.normal(k, shape, F32) * scale

    def gain(k, shape):
        return 1.0 + 0.02 * jax.random.normal(k, shape, F32)

    ac = jax.random.uniform(ks[30], (N_ODD, 2, D_WIDTH), F32, 0.9, 0.999)
    a0 = ac ** (1.0 / LRU_C)
    lam_im0 = jnp.broadcast_to(np.pi * jnp.arange(B_STATE, dtype=F32), (N_EVEN, 2, B_GROUPS, B_STATE))
    return {
        'x_prompt': nrm(ks[0], (BATCH, SEQ, D_MODEL), 1.0),
        'x_sample': nrm(ks[1], (DEC_BATCH, DEC_SEQ, D_MODEL), 1.0),
        'mem_prompt': nrm(ks[2], (BATCH, N_MEM, D_MODEL), 1.0),
        'mem_sample': nrm(ks[3], (DEC_BATCH, N_MEM, D_MODEL), 1.0),
        'norm_mix_pre': gain(ks[4], (DEPTH, D_MODEL)),
        'norm_mix_post': gain(ks[5], (DEPTH, D_MODEL)),
        'norm_x_pre': gain(ks[6], (DEPTH, D_MODEL)),
        'norm_x_post': gain(ks[7], (DEPTH, D_MODEL)),
        'ev_w_in': nrm(ks[8], (N_EVEN, D_MODEL, EVEN_IN), D_MODEL ** -0.5),
        'ev_w_out': nrm(ks[9], (N_EVEN, EVEN_MIX, D_MODEL), EVEN_MIX ** -0.5),
        's5_lam_re': -0.5 + 0.01 * jax.random.normal(ks[10], (N_EVEN, 2, B_GROUPS, B_STATE), F32),
        's5_lam_im': lam_im0 + 0.01 * jax.random.normal(ks[11], (N_EVEN, 2, B_GROUPS, B_STATE), F32),
        's5_log_step': jax.random.uniform(ks[12], (N_EVEN, 2, B_GROUPS), F32, math.log(1e-3), math.log(1e-1)),
        's5_b_re': nrm(ks[13], (N_EVEN, 2, B_GROUPS, B_STATE, B_GROUP), (2 * B_GROUP) ** -0.5),
        's5_b_im': nrm(ks[14], (N_EVEN, 2, B_GROUPS, B_STATE, B_GROUP), (2 * B_GROUP) ** -0.5),
        's5_c_re': nrm(ks[15], (N_EVEN, 2, B_GROUPS, B_GROUP, B_STATE), (2 * B_STATE) ** -0.5),
        's5_c_im': nrm(ks[16], (N_EVEN, 2, B_GROUPS, B_GROUP, B_STATE), (2 * B_STATE) ** -0.5),
        's5_d': nrm(ks[17], (N_EVEN, B_WIDTH), 1.0),
        's5_glu_w': nrm(ks[18], (N_EVEN, B_WIDTH, B_WIDTH), B_WIDTH ** -0.5),
        's5_glu_b': nrm(ks[19], (N_EVEN, B_WIDTH), 0.01),
        'od_w_in': nrm(ks[20], (N_ODD, D_MODEL, ODD_IN), D_MODEL ** -0.5),
        'od_w_out': nrm(ks[21], (N_ODD, ODD_MIX, D_MODEL), ODD_MIX ** -0.5),
        'hgrn_lb_logits': nrm(ks[22], (DEPTH, C_WIDTH), 0.1),
        'hgrn_onorm': gain(ks[23], (N_ODD, C_WIDTH)),
        'lru_conv_w': nrm(ks[24], (N_ODD, D_CONV, D_WIDTH), D_CONV ** -0.5),
        'lru_conv_b': nrm(ks[25], (N_ODD, D_WIDTH), 0.01),
        'lru_wr': nrm(ks[26], (N_ODD, 2, D_BLOCKS, D_BLOCK_DIM, D_BLOCK_DIM), D_BLOCK_DIM ** -0.5),
        'lru_br': nrm(ks[27], (N_ODD, 2, D_WIDTH), 0.01),
        'lru_wi': nrm(ks[28], (N_ODD, 2, D_BLOCKS, D_BLOCK_DIM, D_BLOCK_DIM), D_BLOCK_DIM ** -0.5),
        'lru_bi': nrm(ks[29], (N_ODD, 2, D_WIDTH), 0.01),
        'lru_lambda': jnp.log(a0) - jnp.log1p(-a0),
        'x_wq': nrm(ks[31], (DEPTH, D_MODEL, X_WIDTH), D_MODEL ** -0.5),
        'x_wkv': nrm(ks[32], (DEPTH, D_MODEL, 2 * X_WIDTH), D_MODEL ** -0.5),
        'x_wo': nrm(ks[33], (DEPTH, X_WIDTH, D_MODEL), X_WIDTH ** -0.5),
        'x_mem_norm': gain(ks[34], (DEPTH, D_MODEL)),
    }


def reference(x_prompt, x_sample, mem_prompt, mem_sample, norm_mix_pre, norm_mix_post, norm_x_pre,
              norm_x_post, ev_w_in, ev_w_out, s5_lam_re, s5_lam_im, s5_log_step, s5_b_re, s5_b_im,
              s5_c_re, s5_c_im, s5_d, s5_glu_w, s5_glu_b, od_w_in, od_w_out, hgrn_lb_logits, hgrn_onorm,
              lru_conv_w, lru_conv_b, lru_wr, lru_br, lru_wi, lru_bi, lru_lambda, x_wq, x_wkv, x_wo,
              x_mem_norm):
    params = dict(norm_mix_pre=norm_mix_pre, norm_mix_post=norm_mix_post, norm_x_pre=norm_x_pre,
                  norm_x_post=norm_x_post, ev_w_in=ev_w_in, ev_w_out=ev_w_out, s5_lam_re=s5_lam_re,
                  s5_lam_im=s5_lam_im, s5_log_step=s5_log_step, s5_b_re=s5_b_re, s5_b_im=s5_b_im,
                  s5_c_re=s5_c_re, s5_c_im=s5_c_im, s5_d=s5_d, s5_glu_w=s5_glu_w, s5_glu_b=s5_glu_b,
                  od_w_in=od_w_in, od_w_out=od_w_out, hgrn_lb_logits=hgrn_lb_logits, hgrn_onorm=hgrn_onorm,
                  lru_conv_w=lru_conv_w, lru_conv_b=lru_conv_b, lru_wr=lru_wr, lru_br=lru_br,
                  lru_wi=lru_wi, lru_bi=lru_bi, lru_lambda=lru_lambda, x_wq=x_wq, x_wkv=x_wkv,
                  x_wo=x_wo, x_mem_norm=x_mem_norm)
    y_prompt = _trunk(x_prompt, mem_prompt, params)
    y_sample = _trunk(x_sample, mem_sample, params)
    return (y_prompt, y_sample)
```

```python
import functools
import math

import jax
import jax.numpy as jnp
import numpy as np
from jax import lax
from jax.experimental import pallas as pl
from jax.experimental.pallas import tpu as pltpu

F32 = jnp.float32
BF16 = jnp.bfloat16

D_MODEL = 2048
N_MEM = 256
EPS = 1e-6
NEG_INF = -1e30

A_HEADS = 12
HEAD_DIM = 128
A_WIDTH = A_HEADS * HEAD_DIM
A_DILATIONS = (1, 4, 16)
A_RADIUS = 64
B_WIDTH = 512
B_GROUP = 16
B_GROUPS = B_WIDTH // B_GROUP
B_STATE = 64
B_STATES = B_GROUPS * B_STATE
C_HEADS = 8
C_WIDTH = C_HEADS * HEAD_DIM
C_SUB = 32
D_WIDTH = 1024
D_BLOCKS = 8
D_BLOCK_DIM = D_WIDTH // D_BLOCKS
LRU_C = 8.0
X_HEADS = 4
X_WIDTH = X_HEADS * HEAD_DIM

EVEN_IN = 4 * A_WIDTH + 2 * B_WIDTH
ODD_IN = 5 * C_WIDTH + 2 * D_WIDTH

SUBLANES = 8
LANES = 128
VMEM_LIMIT = 56 * 1024 * 1024


def _params(*sem):
    return pltpu.CompilerParams(dimension_semantics=sem, vmem_limit_bytes=VMEM_LIMIT)


def _rms(x, g):
    return x * lax.rsqrt(jnp.mean(x * x, axis=-1, keepdims=True) + EPS) * g


def _dot(a, b):
    return jnp.dot(a.astype(BF16), b.astype(BF16), preferred_element_type=F32)


def _dot_t(a, b):
    return lax.dot_general(a.astype(BF16), b.astype(BF16), (((1,), (1,)), ((), ())),
                           preferred_element_type=F32)


def _tdot(a, b):
    return lax.dot_general(a.astype(BF16), b.astype(BF16), (((0,), (0,)), ((), ())),
                           preferred_element_type=F32)


def _norm_matmul_kernel(x_ref, g_ref, w_ref, o_ref, xn_ref):
    @pl.when(pl.program_id(2) == 0)
    def _():
        xn_ref[...] = _rms(x_ref[0], g_ref[...]).astype(BF16)

    o_ref[0] = jnp.dot(xn_ref[...], w_ref[...], preferred_element_type=F32).astype(o_ref.dtype)


def _norm_matmul(x, g, w, *, tm, tn, out_dtype=F32):
    bt, s, d = x.shape
    n = w.shape[1]
    return pl.pallas_call(
        _norm_matmul_kernel,
        grid=(bt, s // tm, n // tn),
        in_specs=[
            pl.BlockSpec((1, tm, d), lambda b, i, j: (b, i, 0)),
            pl.BlockSpec((1, d), lambda b, i, j: (0, 0)),
            pl.BlockSpec((d, tn), lambda b, i, j: (0, j)),
        ],
        out_specs=pl.BlockSpec((1, tm, tn), lambda b, i, j: (b, i, j)),
        out_shape=jax.ShapeDtypeStruct((bt, s, n), out_dtype),
        scratch_shapes=[pltpu.VMEM((tm, d), BF16)],
        compiler_params=_params("parallel", "parallel", "arbitrary"),
    )(x, g.reshape(1, d), w)


A_QB = 64
A_KW = 3 * A_QB


def _attn_kernel(slope_ref, q_ref, k_ref, v_ref, g_ref, o_ref, m_ref, l_ref, acc_ref, *, tq, seq):
    t0 = pl.program_id(2) * tq
    slope = slope_ref[0]
    m_ref[...] = jnp.full(m_ref.shape, NEG_INF, F32)
    l_ref[...] = jnp.zeros(l_ref.shape, F32)
    acc_ref[...] = jnp.zeros(acc_ref.shape, F32)
    drel = (lax.broadcasted_iota(jnp.int32, (A_QB, A_KW), 1)
            - lax.broadcasted_iota(jnp.int32, (A_QB, A_KW), 0))

    for dil in A_DILATIONS:
        sub_len = seq // dil
        blocks = tq // (dil * A_QB)

        def block_step(c, carry, dil=dil, sub_len=sub_len):
            for r in range(dil):
                qj = t0 // dil + c * A_QB
                kj = jnp.clip(qj - A_RADIUS, 0, sub_len - A_KW)
                rows = pl.ds(r + pl.multiple_of(c * (A_QB * dil), A_QB), A_QB, stride=dil)
                krows = pl.ds(r + pl.multiple_of(kj * dil, A_QB), A_KW, stride=dil)
                q = q_ref[rows, :] * (HEAD_DIM ** -0.5)
                k = k_ref[krows, :]
                v = v_ref[krows, :]
                dist = jnp.abs(drel + (kj - qj))
                sc = _dot_t(q, k) - slope * (dil * dist).astype(F32)
                sc = jnp.where(dist <= A_RADIUS, sc, NEG_INF)
                m_old = m_ref[rows, :]
                m_new = jnp.maximum(m_old, jnp.max(sc, axis=-1, keepdims=True))
                alpha = jnp.exp(m_old - m_new)
                p = jnp.exp(sc - jnp.concatenate([m_new, m_new[:, :A_KW - HEAD_DIM]], axis=1))
                l_ref[rows, :] = alpha * l_ref[rows, :] + jnp.sum(p, axis=-1, keepdims=True)
                acc_ref[rows, :] = alpha * acc_ref[rows, :] + _dot(p, v)
                m_ref[rows, :] = m_new
            return carry

        lax.fori_loop(jnp.int32(0), jnp.int32(blocks), block_step, 0)

    o_ref[...] = (acc_ref[...] / l_ref[...] * _silu(g_ref[...])).astype(o_ref.dtype)


def _dilated_attention(z, slopes, *, tq):
    bt, s, _ = z.shape
    assert s % tq == 0 and tq % (A_DILATIONS[-1] * A_QB) == 0 and s // A_DILATIONS[-1] >= A_KW
    kernel = functools.partial(_attn_kernel, tq=tq, seq=s)
    return pl.pallas_call(
        kernel,
        grid=(bt, A_HEADS, s // tq),
        in_specs=[
            pl.BlockSpec((1, 1, 1), lambda b, h, i: (h, 0, 0)),
            pl.BlockSpec((None, tq, HEAD_DIM), lambda b, h, i: (b, i, h)),
            pl.BlockSpec((None, s, HEAD_DIM), lambda b, h, i: (b, 0, A_HEADS + h)),
            pl.BlockSpec((None, s, HEAD_DIM), lambda b, h, i: (b, 0, 2 * A_HEADS + h)),
            pl.BlockSpec((None, tq, HEAD_DIM), lambda b, h, i: (b, i, 3 * A_HEADS + h)),
        ],
        out_specs=pl.BlockSpec((None, tq, HEAD_DIM), lambda b, h, i: (b, i, h)),
        out_shape=jax.ShapeDtypeStruct((bt, s, A_WIDTH), F32),
        scratch_shapes=[pltpu.VMEM((tq, HEAD_DIM), F32)] * 3,
        compiler_params=_params("parallel", "parallel", "arbitrary"),
    )(slopes.reshape(A_HEADS, 1, 1), z, z, z, z)


S5_COLS = 512
SCAN_UNROLL = 4


def _load_tiles(refs, rows):
    return jnp.concatenate([ref[rows, :] for ref in refs], axis=1)


def _store_tiles(refs, rows, val):
    for j, ref in enumerate(refs):
        ref[rows, :] = val[:, j * LANES:(j + 1) * LANES]


def _s5_scan(bu_refs, a_ref, carry_ref, *, ts, reverse):
    run = ts // SUBLANES
    sub = lax.broadcasted_iota(jnp.int32, (SUBLANES, S5_COLS), 0)
    for c0 in range(0, B_STATES, S5_COLS):
        re = pl.ds(c0, S5_COLS)
        im = pl.ds(B_STATES + c0, S5_COLS)
        re_refs = bu_refs[c0 // LANES:(c0 + S5_COLS) // LANES]
        im_refs = bu_refs[(B_STATES + c0) // LANES:(B_STATES + c0 + S5_COLS) // LANES]
        ar = jnp.broadcast_to(a_ref[0:1, re], (SUBLANES, S5_COLS))
        ai = jnp.broadcast_to(a_ref[0:1, im], (SUBLANES, S5_COLS))

        def sweep(xr, xi, store):
            def step(n, x):
                xr, xi = x
                rows = pl.ds(run - 1 - n if reverse else n, SUBLANES, stride=run)
                nr = ar * xr - ai * xi + _load_tiles(re_refs, rows)
                ni = ar * xi + ai * xr + _load_tiles(im_refs, rows)
                if store:
                    _store_tiles(re_refs, rows, nr)
                    _store_tiles(im_refs, rows, ni)
                return nr, ni
            return lax.fori_loop(0, run, step, (xr, xi), unroll=SCAN_UNROLL)

        zero = jnp.zeros((SUBLANES, S5_COLS), F32)
        er, ei = sweep(zero, zero, False)
        pr, pi = a_ref[1:2, re], a_ref[1:2, im]
        cr, ci = carry_ref[0:1, re], carry_ref[0:1, im]
        startr, starti = zero, zero
        for s_ in (range(SUBLANES - 1, -1, -1) if reverse else range(SUBLANES)):
            startr = jnp.where(sub == s_, cr, startr)
            starti = jnp.where(sub == s_, ci, starti)
            nr = pr * cr - pi * ci + er[s_:s_ + 1]
            ni = pr * ci + pi * cr + ei[s_:s_ + 1]
            cr, ci = nr, ni
        carry_ref[0:1, re] = cr
        carry_ref[0:1, im] = ci
        sweep(startr, starti, True)


def _s5_kernel(uf_ref, ub_ref, wb_ref, a_ref, wc_ref, yf_ref, yb_ref, carry_ref, *bu_refs, ts):
    @pl.when(pl.program_id(1) == 0)
    def _():
        carry_ref[...] = jnp.zeros(carry_ref.shape, F32)

    all_rows = pl.ds(0, ts)
    for di, (u_ref, y_ref) in enumerate(((uf_ref, yf_ref), (ub_ref, yb_ref))):
        _store_tiles(bu_refs, all_rows, _dot(u_ref[0], wb_ref[di]))
        _s5_scan(bu_refs, a_ref.at[di], carry_ref.at[di], ts=ts, reverse=(di == 1))
        y_ref[0] = _dot(_load_tiles(bu_refs, all_rows), wc_ref[di])


def _s5_tables(lam_re, lam_im, log_step, b_re, b_im, c_re, c_im, run):
    lr, li = lam_re.astype(F32), lam_im.astype(F32)
    dt = jnp.exp(log_step.astype(F32))[..., None]
    e = jnp.exp(lr * dt)
    abar_re, abar_im = e * jnp.cos(li * dt), e * jnp.sin(li * dt)
    den = lr * lr + li * li
    nr, ni = abar_re - 1.0, abar_im
    fr, fi = (nr * lr + ni * li) / den, (ni * lr - nr * li) / den
    br, bi = b_re.astype(F32), b_im.astype(F32)
    bbar_re = fr[..., None] * br - fi[..., None] * bi
    bbar_im = fr[..., None] * bi + fi[..., None] * br
    eye = jnp.eye(B_GROUPS, dtype=F32)
    wb = jnp.concatenate([jnp.einsum('dgpc,gh->dgchp', t, eye).reshape(2, B_WIDTH, B_STATES)
                          for t in (bbar_re, bbar_im)], axis=-1)
    wc = jnp.concatenate([jnp.einsum('dgcp,gh->dhpgc', t, eye).reshape(2, B_STATES, B_WIDTH)
                          for t in (c_re.astype(F32), -c_im.astype(F32))], axis=1)
    pr, pi = abar_re, abar_im
    for _ in range(run - 1):
        pr, pi = pr * abar_re - pi * abar_im, pr * abar_im + pi * abar_re
    a = jnp.stack([jnp.concatenate([abar_re.reshape(2, B_STATES), abar_im.reshape(2, B_STATES)], axis=-1),
                   jnp.concatenate([pr.reshape(2, B_STATES), pi.reshape(2, B_STATES)], axis=-1)], axis=1)
    return wb.astype(BF16), a, wc.astype(BF16)


def _s5(z, tables, *, ts):
    bt, s, _ = z.shape
    wb, a, wc = tables
    nt = s // ts
    ucol = 4 * A_WIDTH // B_WIDTH
    kernel = functools.partial(_s5_kernel, ts=ts)
    full = lambda shape: pl.BlockSpec(shape, lambda b, i: (0,) * len(shape))
    return pl.pallas_call(
        kernel,
        grid=(bt, nt),
        in_specs=[
            pl.BlockSpec((1, ts, B_WIDTH), lambda b, i: (b, i, ucol)),
            pl.BlockSpec((1, ts, B_WIDTH), lambda b, i: (b, nt - 1 - i, ucol)),
            full(wb.shape), full(a.shape), full(wc.shape),
        ],
        out_specs=[
            pl.BlockSpec((1, ts, B_WIDTH), lambda b, i: (b, i, 0)),
            pl.BlockSpec((1, ts, B_WIDTH), lambda b, i: (b, nt - 1 - i, 0)),
        ],
        out_shape=[jax.ShapeDtypeStruct((bt, s, B_WIDTH), F32)] * 2,
        scratch_shapes=([pltpu.VMEM((2, 1, 2 * B_STATES), F32)]
                        + [pltpu.VMEM((ts, LANES), F32)] * (2 * B_STATES // LANES)),
        compiler_params=_params("parallel", "arbitrary"),
    )(z, z, wb, a, wc)


def _gelu_tanh(x):
    return 0.5 * x * (1.0 + jnp.tanh(math.sqrt(2.0 / math.pi) * (x + 0.044715 * (x * x * x))))


def _silu(x):
    return x * jax.nn.sigmoid(x)


def _even_out_kernel(h_ref, ya_ref, yf_ref, yb_ref, u_ref, gb_ref, d_ref, gw_ref, gbias_ref, wa_ref, wb_ref,
                     gpost_ref, o_ref):
    y_s5 = _gelu_tanh(d_ref[...] * u_ref[0] + yf_ref[0] + yb_ref[0])
    y_glu = y_s5 * jax.nn.sigmoid(jnp.dot(y_s5.astype(BF16), gw_ref[...], preferred_element_type=F32)
                                  + gbias_ref[...])
    y_b = y_glu * _silu(gb_ref[0])
    y = (jnp.dot(ya_ref[0].astype(BF16), wa_ref[...], preferred_element_type=F32)
         + jnp.dot(y_b.astype(BF16), wb_ref[...], preferred_element_type=F32))
    o_ref[0] = h_ref[0] + _rms(y, gpost_ref[...])


def _even_out(h, y_a, y_f, y_b, z, d_skip, glu_w, glu_b, w_out, g_post, *, tm):
    bt, s, d = h.shape
    ucol = 4 * A_WIDTH // B_WIDTH
    row = lambda width, col=0: pl.BlockSpec((1, tm, width), lambda b, i: (b, i, col))
    full = lambda shape: pl.BlockSpec(shape, lambda b, i: (0,) * len(shape))
    return pl.pallas_call(
        _even_out_kernel,
        grid=(bt, s // tm),
        in_specs=[row(d), row(A_WIDTH), row(B_WIDTH), row(B_WIDTH), row(B_WIDTH, ucol), row(B_WIDTH, ucol + 1),
                  full((1, B_WIDTH)), full((B_WIDTH, B_WIDTH)), full((1, B_WIDTH)),
                  full((A_WIDTH, d)), full((B_WIDTH, d)), full((1, d))],
        out_specs=row(d),
        out_shape=jax.ShapeDtypeStruct((bt, s, d), F32),
        compiler_params=_params("parallel", "parallel"),
    )(h, y_a, y_f, y_b, z, z, d_skip.reshape(1, -1), glu_w, glu_b.reshape(1, -1),
      w_out[:A_WIDTH], w_out[A_WIDTH:], g_post.reshape(1, -1))


def _odd_out_kernel(h_ref, of_ref, ob_ref, gc_ref, hf_ref, hb_ref, gd_ref, onorm_ref, wc_ref, wd_ref, gpost_ref,
                    o_ref):
    o_c = of_ref[0] + ob_ref[0]
    heads = [o_c[:, hd * HEAD_DIM:(hd + 1) * HEAD_DIM] for hd in range(C_HEADS)]
    o_c = jnp.concatenate([t * lax.rsqrt(jnp.mean(t * t, axis=-1, keepdims=True) + EPS) for t in heads], axis=-1)
    y_c = o_c * onorm_ref[...] * _silu(gc_ref[0])
    y_d = (hf_ref[0] + hb_ref[0]) * _silu(gd_ref[0])
    y = (jnp.dot(y_c.astype(BF16), wc_ref[...], preferred_element_type=F32)
         + jnp.dot(y_d.astype(BF16), wd_ref[...], preferred_element_type=F32))
    o_ref[0] = h_ref[0] + _rms(y, gpost_ref[...])


def _odd_out(h, o_f, o_b, h_f, h_b, z, onorm, w_out, g_post, *, tm):
    bt, s, d = h.shape
    gccol = 4 * C_WIDTH // C_WIDTH
    gdcol = (5 * C_WIDTH + D_WIDTH) // D_WIDTH
    row = lambda width, col=0: pl.BlockSpec((1, tm, width), lambda b, i: (b, i, col))
    full = lambda shape: pl.BlockSpec(shape, lambda b, i: (0,) * len(shape))
    return pl.pallas_call(
        _odd_out_kernel,
        grid=(bt, s // tm),
        in_specs=[row(d), row(C_WIDTH), row(C_WIDTH), row(C_WIDTH, gccol),
                  row(D_WIDTH), row(D_WIDTH), row(D_WIDTH, gdcol),
                  full((1, C_WIDTH)), full((C_WIDTH, d)), full((D_WIDTH, d)), full((1, d))],
        out_specs=row(d),
        out_shape=jax.ShapeDtypeStruct((bt, s, d), F32),
        compiler_params=_params("parallel", "parallel"),
    )(h, o_f, o_b, z, h_f, h_b, z, onorm.reshape(1, -1), w_out[:C_WIDTH], w_out[C_WIDTH:],
      g_post.reshape(1, -1))


def _xattn_kernel(h_ref, gpre_ref, wq_ref, kv_ref, wo_ref, gpost_ref, o_ref):
    h = h_ref[0]
    q = _dot(_rms(h, gpre_ref[...]), wq_ref[...]) * (HEAD_DIM ** -0.5)
    heads = []
    for hd in range(X_HEADS):
        cols = slice(hd * HEAD_DIM, (hd + 1) * HEAD_DIM)
        k = kv_ref[0, :, cols]
        v = kv_ref[0, :, X_WIDTH + hd * HEAD_DIM:X_WIDTH + (hd + 1) * HEAD_DIM]
        sc = _dot_t(q[:, cols], k)
        p = jnp.exp(sc - jnp.max(sc, axis=-1, keepdims=True))
        heads.append(_dot(p, v) / jnp.sum(p, axis=-1, keepdims=True))
    o = jnp.concatenate(heads, axis=-1)
    o_ref[0] = h + _rms(_dot(o, wo_ref[...]), gpost_ref[...])


def _xattn(h, kv, g_pre, wq, wo, g_post, *, tm):
    bt, s, d = h.shape
    full = lambda shape: pl.BlockSpec(shape, lambda b, i: (0,) * len(shape))
    return pl.pallas_call(
        _xattn_kernel,
        grid=(bt, s // tm),
        in_specs=[pl.BlockSpec((1, tm, d), lambda b, i: (b, i, 0)), full((1, d)), full((d, X_WIDTH)),
                  pl.BlockSpec((1, N_MEM, 2 * X_WIDTH), lambda b, i: (b, 0, 0)),
                  full((X_WIDTH, d)), full((1, d))],
        out_specs=pl.BlockSpec((1, tm, d), lambda b, i: (b, i, 0)),
        out_shape=jax.ShapeDtypeStruct((bt, s, d), F32),
        compiler_params=_params("parallel", "parallel"),
    )(h, g_pre.reshape(1, -1), wq, kv, wo, g_post.reshape(1, -1))


def _hgrn_kernel(q_ref, f_ref, v_ref, lb_ref, o_ref, state_ref, *, th, reverse):
    @pl.when(pl.program_id(2) == 0)
    def _():
        state_ref[...] = jnp.zeros(state_ref.shape, F32)

    lb = lb_ref[...]
    fx = f_ref[0]
    qf = _silu(q_ref[0])
    v = v_ref[0]
    logf = jnp.log(lb + (1.0 - lb) * jax.nn.sigmoid(fx))
    kf = (1.0 - lb) * jax.nn.sigmoid(-fx)

    t = lax.broadcasted_iota(jnp.int32, (th, th), 0)
    r = lax.broadcasted_iota(jnp.int32, (th, th), 1)
    same = (t // C_SUB) == (r // C_SUB)
    half = C_SUB // 2
    if reverse:
        causal = same & (r >= t)
        first = same & (r % C_SUB >= half)
    else:
        causal = same & (r <= t)
        first = same & (r % C_SUB < half)
    hi = logf.astype(BF16)
    lo = (logf - hi.astype(F32)).astype(BF16)

    def rowsum(mask):
        m = mask.astype(BF16)
        return (jnp.dot(m, hi, preferred_element_type=F32) + jnp.dot(m, lo, preferred_element_type=F32))

    cum = rowsum(causal)
    mid = rowsum(first)
    tot = rowsum(same)
    att = jnp.where(causal, _dot_t(qf * jnp.exp(cum - mid), kf * jnp.exp(mid - cum)), 0.0)
    o_ref[0] = _dot(att, v)
    q_in = qf * jnp.exp(cum)
    k_out = kf * jnp.exp(tot - cum)

    nblk = th // C_SUB
    for j in (range(nblk - 1, -1, -1) if reverse else range(nblk)):
        rows = slice(j * C_SUB, (j + 1) * C_SUB)
        state = state_ref[...]
        o_ref[0, rows, :] += _dot_t(q_in[rows], state)
        state_ref[...] = state * jnp.exp(tot[j * C_SUB:j * C_SUB + 1]) + _tdot(v[rows], k_out[rows])


def _hgrn(z, lb, *, th, reverse):
    bt, s, _ = z.shape
    nt = s // th
    fcol = 2 * C_HEADS if reverse else C_HEADS
    tile = (lambda i: nt - 1 - i) if reverse else (lambda i: i)
    col = lambda c0: pl.BlockSpec((1, th, HEAD_DIM), lambda b, h, i: (b, tile(i), c0 + h))
    kernel = functools.partial(_hgrn_kernel, th=th, reverse=reverse)
    return pl.pallas_call(
        kernel,
        grid=(bt, C_HEADS, nt),
        in_specs=[col(0), col(fcol), col(3 * C_HEADS), pl.BlockSpec((1, HEAD_DIM), lambda b, h, i: (0, h))],
        out_specs=col(0),
        out_shape=jax.ShapeDtypeStruct((bt, s, C_WIDTH), F32),
        scratch_shapes=[pltpu.VMEM((HEAD_DIM, HEAD_DIM), F32)],
        compiler_params=_params("parallel", "parallel", "arbitrary"),
    )(z, z, z, lb.reshape(1, C_WIDTH))


D_HALO = SUBLANES
LRU_COLS = 512


def _lru_scan(a_refs, bx_refs, carry_ref, *, tl, reverse):
    run = tl // SUBLANES
    sub = lax.broadcasted_iota(jnp.int32, (SUBLANES, LRU_COLS), 0)
    for c0 in range(0, D_WIDTH, LRU_COLS):
        cols = pl.ds(c0, LRU_COLS)
        a_tiles = a_refs[c0 // LANES:(c0 + LRU_COLS) // LANES]
        bx_tiles = bx_refs[c0 // LANES:(c0 + LRU_COLS) // LANES]

        def rows_at(n):
            return pl.ds(run - 1 - n if reverse else n, SUBLANES, stride=run)

        def first(n, x):
            hcur, prod = x
            a = _load_tiles(a_tiles, rows_at(n))
            return a * hcur + _load_tiles(bx_tiles, rows_at(n)), a * prod

        def second(n, hcur):
            hcur = _load_tiles(a_tiles, rows_at(n)) * hcur + _load_tiles(bx_tiles, rows_at(n))
            _store_tiles(bx_tiles, rows_at(n), hcur)
            return hcur

        zero = jnp.zeros((SUBLANES, LRU_COLS), F32)
        hcur, prod = lax.fori_loop(0, run, first, (zero, zero + 1.0), unroll=SCAN_UNROLL)
        c = carry_ref[0:1, cols]
        start = zero
        for s_ in (range(SUBLANES - 1, -1, -1) if reverse else range(SUBLANES)):
            start = jnp.where(sub == s_, c, start)
            c = prod[s_:s_ + 1] * c + hcur[s_:s_ + 1]
        carry_ref[0:1, cols] = c
        lax.fori_loop(0, run, second, start, unroll=SCAN_UNROLL)


def _lru_kernel(xf_ref, xfp_ref, xfn_ref, xb_ref, xbp_ref, xbn_ref, cw_ref, cb_ref, w_ref, bias_ref, sp_ref,
                hf_ref, hb_ref, carry_ref, *ab_refs, tl, nt):
    i = pl.program_id(1)
    a_refs, bx_refs = ab_refs[:D_WIDTH // LANES], ab_refs[D_WIDTH // LANES:]
    all_rows = pl.ds(0, tl)

    @pl.when(i == 0)
    def _():
        carry_ref[...] = jnp.zeros(carry_ref.shape, F32)

    dirs = ((xf_ref, xfp_ref, xfn_ref, hf_ref, i), (xb_ref, xbp_ref, xbn_ref, hb_ref, nt - 1 - i))
    for di, (x_ref, xp_ref, xn_ref, h_ref, tile) in enumerate(dirs):
        x = x_ref[0]
        prev = jnp.where(tile > 0, xp_ref[0], 0.0)
        nxt = jnp.where(tile < nt - 1, xn_ref[0], 0.0)
        ext = jnp.concatenate([prev, x, nxt], axis=0)
        xc = cb_ref[...]
        for tap in range(4):
            xc = xc + cw_ref[tap:tap + 1, :] * ext[D_HALO - 1 + tap:D_HALO - 1 + tap + tl]
        gates = []
        for n in range(D_BLOCKS):
            cols = slice(n * D_BLOCK_DIM, (n + 1) * D_BLOCK_DIM)
            gates.append(_dot(xc[:, cols], w_ref[di, n]))
        r = jax.nn.sigmoid(jnp.concatenate([g[:, :D_BLOCK_DIM] for g in gates], axis=-1) + bias_ref[di, 0:1])
        ig = jax.nn.sigmoid(jnp.concatenate([g[:, D_BLOCK_DIM:] for g in gates], axis=-1) + bias_ref[di, 1:2])
        log_a = -LRU_C * r * sp_ref[di:di + 1]
        a = jnp.exp(log_a)
        _store_tiles(a_refs, all_rows, a)
        _store_tiles(bx_refs, all_rows, jnp.sqrt(-jnp.tanh(log_a) * (a * a + 1.0)) * (ig * xc))
        _lru_scan(a_refs, bx_refs, carry_ref.at[di], tl=tl, reverse=(di == 1))
        h_ref[0] = _load_tiles(bx_refs, all_rows)


def _rglru(z, conv_w, conv_b, wr, br, wi, bi, lam, *, tl):
    bt, s, _ = z.shape
    nt = s // tl
    xcol = 5 * C_WIDTH // D_WIDTH
    hpt = tl // D_HALO
    nh = s // D_HALO
    w = jnp.concatenate([wr, wi], axis=-1).astype(BF16)
    bias = jnp.stack([br, bi], axis=1).astype(F32)
    sp = jax.nn.softplus(-lam.astype(F32))
    fwd = lambda i: i
    bwd = lambda i: nt - 1 - i
    specs = []
    for tile in (fwd, bwd):
        specs += [
            pl.BlockSpec((1, tl, D_WIDTH), lambda b, i, tile=tile: (b, tile(i), xcol)),
            pl.BlockSpec((1, D_HALO, D_WIDTH),
                         lambda b, i, tile=tile: (b, jnp.maximum(tile(i) * hpt - 1, 0), xcol)),
            pl.BlockSpec((1, D_HALO, D_WIDTH),
                         lambda b, i, tile=tile: (b, jnp.minimum((tile(i) + 1) * hpt, nh - 1), xcol)),
        ]
    full = lambda shape: pl.BlockSpec(shape, lambda b, i: (0,) * len(shape))
    kernel = functools.partial(_lru_kernel, tl=tl, nt=nt)
    return pl.pallas_call(
        kernel,
        grid=(bt, nt),
        in_specs=specs + [full((4, D_WIDTH)), full((1, D_WIDTH)), full(w.shape), full(bias.shape), full(sp.shape)],
        out_specs=[pl.BlockSpec((1, tl, D_WIDTH), lambda b, i: (b, i, 0)),
                   pl.BlockSpec((1, tl, D_WIDTH), lambda b, i: (b, nt - 1 - i, 0))],
        out_shape=[jax.ShapeDtypeStruct((bt, s, D_WIDTH), F32)] * 2,
        scratch_shapes=([pltpu.VMEM((2, 1, D_WIDTH), F32)]
                        + [pltpu.VMEM((tl, LANES), F32)] * (2 * D_WIDTH // LANES)),
        compiler_params=_params("parallel", "arbitrary"),
    )(z, z, z, z, z, z, conv_w.astype(F32), conv_b.reshape(1, -1).astype(F32), w, bias, sp)


TM_PROJ = 512
TN_PROJ = 1024
TQ_ATTN = 1024
TS_S5 = 256
TM_OUT = 256
TH_HGRN = 256
TL_LRU = 256


def _trunk(x, mem, p):
    bf = lambda t: t.astype(BF16)
    lb_all = jnp.cumsum(jax.nn.softmax(p['hgrn_lb_logits'].astype(F32), axis=0), axis=0)
    lb_all = lb_all - lb_all[0:1]
    slopes = jnp.asarray(2.0 ** (-8.0 * np.arange(1, A_HEADS + 1) / A_HEADS), dtype=F32)
    depth = p['norm_mix_pre'].shape[0]
    h = x
    for l in range(depth):
        if l % 2 == 0:
            e = l // 2
            z = _norm_matmul(h, p['norm_mix_pre'][l], bf(p['ev_w_in'][e]), tm=TM_PROJ, tn=TN_PROJ)
            y_a = _dilated_attention(z, slopes, tq=TQ_ATTN)
            tables = _s5_tables(p['s5_lam_re'][e], p['s5_lam_im'][e], p['s5_log_step'][e], p['s5_b_re'][e],
                                p['s5_b_im'][e], p['s5_c_re'][e], p['s5_c_im'][e], TS_S5 // SUBLANES)
            y_f, y_b = _s5(z, tables, ts=TS_S5)
            h = _even_out(h, y_a, y_f, y_b, z, p['s5_d'][e].astype(F32), bf(p['s5_glu_w'][e]),
                          p['s5_glu_b'][e].astype(F32), bf(p['ev_w_out'][e]), p['norm_mix_post'][l], tm=TM_OUT)
        else:
            o = l // 2
            z = _norm_matmul(h, p['norm_mix_pre'][l], bf(p['od_w_in'][o]), tm=TM_PROJ, tn=TN_PROJ)
            o_f = _hgrn(z, lb_all[l], th=TH_HGRN, reverse=False)
            o_b = _hgrn(z, lb_all[l], th=TH_HGRN, reverse=True)
            h_f, h_b = _rglru(z, p['lru_conv_w'][o], p['lru_conv_b'][o], p['lru_wr'][o], p['lru_br'][o],
                              p['lru_wi'][o], p['lru_bi'][o], p['lru_lambda'][o], tl=TL_LRU)
            h = _odd_out(h, o_f, o_b, h_f, h_b, z, p['hgrn_onorm'][o].astype(F32), bf(p['od_w_out'][o]),
                         p['norm_mix_post'][l], tm=TM_OUT)
        kv = _norm_matmul(mem, p['x_mem_norm'][l], bf(p['x_wkv'][l]), tm=N_MEM, tn=2 * X_WIDTH)
        h = _xattn(h, kv, p['norm_x_pre'][l], bf(p['x_wq'][l]), bf(p['x_wo'][l]), p['norm_x_post'][l], tm=TM_OUT)
    return h


def kernel(x_prompt, x_sample, mem_prompt, mem_sample, norm_mix_pre, norm_mix_post, norm_x_pre, norm_x_post, ev_w_in, ev_w_out, s5_lam_re, s5_lam_im, s5_log_step, s5_b_re, s5_b_im, s5_c_re, s5_c_im, s5_d, s5_glu_w, s5_glu_b, od_w_in, od_w_out, hgrn_lb_logits, hgrn_onorm, lru_conv_w, lru_conv_b, lru_wr, lru_br, lru_wi, lru_bi, lru_lambda, x_wq, x_wkv, x_wo, x_mem_norm):
    params = dict(norm_mix_pre=norm_mix_pre, norm_mix_post=norm_mix_post, norm_x_pre=norm_x_pre,
                  norm_x_post=norm_x_post, ev_w_in=ev_w_in, ev_w_out=ev_w_out, s5_lam_re=s5_lam_re,
                  s5_lam_im=s5_lam_im, s5_log_step=s5_log_step, s5_b_re=s5_b_re, s5_b_im=s5_b_im,
                  s5_c_re=s5_c_re, s5_c_im=s5_c_im, s5_d=s5_d, s5_glu_w=s5_glu_w, s5_glu_b=s5_glu_b,
                  od_w_in=od_w_in, od_w_out=od_w_out, hgrn_lb_logits=hgrn_lb_logits, hgrn_onorm=hgrn_onorm,
                  lru_conv_w=lru_conv_w, lru_conv_b=lru_conv_b, lru_wr=lru_wr, lru_br=lru_br,
                  lru_wi=lru_wi, lru_bi=lru_bi, lru_lambda=lru_lambda, x_wq=x_wq, x_wkv=x_wkv,
                  x_wo=x_wo, x_mem_norm=x_mem_norm)
    return (_trunk(x_prompt, mem_prompt, params), _trunk(x_sample, mem_sample, params))
```

```python
import functools
import math

import jax
import jax.numpy as jnp
import numpy as np
from jax import lax
from jax.experimental import pallas as pl
from jax.experimental.pallas import tpu as pltpu

F32 = jnp.float32
BF16 = jnp.bfloat16

D_MODEL = 2048
N_MEM = 256
EPS = 1e-6
NEG_INF = -1e30

A_HEADS = 12
HEAD_DIM = 128
A_WIDTH = A_HEADS * HEAD_DIM
A_DILATIONS = (1, 4, 16)
A_RADIUS = 64
B_WIDTH = 512
B_GROUP = 16
B_GROUPS = B_WIDTH // B_GROUP
B_STATE = 64
B_STATES = B_GROUPS * B_STATE
C_HEADS = 8
C_WIDTH = C_HEADS * HEAD_DIM
C_SUB = 32
C_HEADS_PER_STEP = 8
D_WIDTH = 1024
D_BLOCKS = 8
D_BLOCK_DIM = D_WIDTH // D_BLOCKS
LRU_C = 8.0
X_HEADS = 4
X_WIDTH = X_HEADS * HEAD_DIM

EVEN_IN = 4 * A_WIDTH + 2 * B_WIDTH
ODD_IN = 5 * C_WIDTH + 2 * D_WIDTH

SUBLANES = 8
LANES = 128
VMEM_LIMIT = 56 * 1024 * 1024


def _params(*sem):
    return pltpu.CompilerParams(dimension_semantics=sem, vmem_limit_bytes=VMEM_LIMIT)


def _rms(x, g):
    return x * lax.rsqrt(jnp.mean(x * x, axis=-1, keepdims=True) + EPS) * g


def _dot(a, b):
    return jnp.dot(a.astype(BF16), b.astype(BF16), preferred_element_type=F32)


def _dot_t(a, b):
    return lax.dot_general(a.astype(BF16), b.astype(BF16), (((1,), (1,)), ((), ())),
                           preferred_element_type=F32)


def _tdot(a, b):
    return lax.dot_general(a.astype(BF16), b.astype(BF16), (((0,), (0,)), ((), ())),
                           preferred_element_type=F32)


def _norm_matmul_kernel(x_ref, g_ref, w_ref, o_ref, xn_ref):
    @pl.when(pl.program_id(2) == 0)
    def _():
        xn_ref[...] = _rms(x_ref[0], g_ref[...]).astype(BF16)

    o_ref[0] = jnp.dot(xn_ref[...], w_ref[...], preferred_element_type=F32).astype(o_ref.dtype)


def _norm_matmul(x, g, w, *, tm, tn, out_dtype=F32):
    bt, s, d = x.shape
    n = w.shape[1]
    return pl.pallas_call(
        _norm_matmul_kernel,
        grid=(bt, s // tm, n // tn),
        in_specs=[
            pl.BlockSpec((1, tm, d), lambda b, i, j: (b, i, 0)),
            pl.BlockSpec((1, d), lambda b, i, j: (0, 0)),
            pl.BlockSpec((d, tn), lambda b, i, j: (0, j)),
        ],
        out_specs=pl.BlockSpec((1, tm, tn), lambda b, i, j: (b, i, j)),
        out_shape=jax.ShapeDtypeStruct((bt, s, n), out_dtype),
        scratch_shapes=[pltpu.VMEM((tm, d), BF16)],
        compiler_params=_params("parallel", "parallel", "arbitrary"),
    )(x, g.reshape(1, d), w)


A_QB = 64
A_KW = 3 * A_QB
A_UNROLL = 4


def _attn_kernel(slope_ref, q_ref, k_ref, v_ref, g_ref, o_ref, *stat_refs, tq, seq):
    t0 = pl.program_id(2) * tq
    slope = slope_ref[0]
    drel = (lax.broadcasted_iota(jnp.int32, (A_QB, A_KW), 1)
            - lax.broadcasted_iota(jnp.int32, (A_QB, A_KW), 0))
    stats = [stat_refs[3 * n:3 * n + 3] for n in range(len(A_DILATIONS))]

    for dil, (m_ref, l_ref, acc_ref) in zip(A_DILATIONS, stats):
        sub_len = seq // dil
        blocks = tq // (dil * A_QB)

        def steps(units, dil=dil, sub_len=sub_len, m_ref=m_ref, l_ref=l_ref, acc_ref=acc_ref):
            staged = []
            for r, c in units:
                qj = t0 // dil + c * A_QB
                kj = jnp.clip(qj - A_RADIUS, 0, sub_len - A_KW)
                rows = pl.ds(r + pl.multiple_of(c * (A_QB * dil), A_QB), A_QB, stride=dil)
                krows = pl.ds(r + pl.multiple_of(kj * dil, A_QB), A_KW, stride=dil)
                sc = _dot_t(q_ref[rows, :] * (HEAD_DIM ** -0.5), k_ref[krows, :])
                staged.append((rows, krows, kj - qj, sc))
            probs = []
            for rows, krows, shift, sc in staged:
                dist = jnp.abs(drel + shift)
                sc = jnp.where(dist <= A_RADIUS, sc - slope * (dil * dist).astype(F32), NEG_INF)
                m = jnp.max(sc, axis=-1, keepdims=True)
                p = jnp.exp(sc - m)
                probs.append((m, jnp.sum(p, axis=-1, keepdims=True), _dot(p, v_ref[krows, :])))
            for (rows, _, _, _), (m, l, acc) in zip(staged, probs):
                m_ref[rows, :] = jnp.broadcast_to(m, (A_QB, HEAD_DIM))
                l_ref[rows, :] = jnp.broadcast_to(l, (A_QB, HEAD_DIM))
                acc_ref[rows, :] = acc

        if dil <= A_UNROLL:
            per = A_UNROLL // dil

            def group_step(gi, carry, dil=dil, per=per, steps=steps):
                steps([(r, gi * per + cc) for cc in range(per) for r in range(dil)])
                return carry

            lax.fori_loop(0, blocks // per, group_step, 0)
        else:
            def block_step(c, carry, dil=dil, steps=steps):
                for r0 in range(0, dil, A_UNROLL):
                    steps([(r0 + u, c) for u in range(A_UNROLL)])
                return carry

            lax.fori_loop(jnp.int32(0), jnp.int32(blocks), block_step, 0)

    m_max = functools.reduce(jnp.maximum, [m_ref[...] for m_ref, _, _ in stats])
    num = 0.0
    den = 0.0
    for m_ref, l_ref, acc_ref in stats:
        w = jnp.exp(m_ref[...] - m_max)
        num = num + w * acc_ref[...]
        den = den + w * l_ref[...]
    o_ref[...] = (num / den * _silu(g_ref[...])).astype(o_ref.dtype)


def _dilated_attention(z, slopes, *, tq):
    bt, s, _ = z.shape
    assert s % tq == 0 and tq % (A_DILATIONS[-1] * A_QB) == 0 and s // A_DILATIONS[-1] >= A_KW
    kernel = functools.partial(_attn_kernel, tq=tq, seq=s)
    return pl.pallas_call(
        kernel,
        grid=(bt, A_HEADS, s // tq),
        in_specs=[
            pl.BlockSpec((1, 1, 1), lambda b, h, i: (h, 0, 0)),
            pl.BlockSpec((None, tq, HEAD_DIM), lambda b, h, i: (b, i, h)),
            pl.BlockSpec((None, s, HEAD_DIM), lambda b, h, i: (b, 0, A_HEADS + h)),
            pl.BlockSpec((None, s, HEAD_DIM), lambda b, h, i: (b, 0, 2 * A_HEADS + h)),
            pl.BlockSpec((None, tq, HEAD_DIM), lambda b, h, i: (b, i, 3 * A_HEADS + h)),
        ],
        out_specs=pl.BlockSpec((None, tq, HEAD_DIM), lambda b, h, i: (b, i, h)),
        out_shape=jax.ShapeDtypeStruct((bt, s, A_WIDTH), F32),
        scratch_shapes=[pltpu.VMEM((tq, HEAD_DIM), F32)] * (3 * len(A_DILATIONS)),
        compiler_params=_params("parallel", "parallel", "arbitrary"),
    )(slopes.reshape(A_HEADS, 1, 1), z, z, z, z)


S5_COLS = 512
SCAN_UNROLL = 4


def _run_major(n, from_time):
    a = lax.broadcasted_iota(jnp.int32, (n, n), 0)
    b = lax.broadcasted_iota(jnp.int32, (n, n), 1)
    pos, t = (a, b) if from_time else (b, a)
    return (t == (pos % SUBLANES) * (n // SUBLANES) + pos // SUBLANES).astype(BF16)


def _permute_rows(m, x):
    hi = x.astype(BF16)
    lo = (x - hi.astype(F32)).astype(BF16)
    return jnp.dot(m, hi, preferred_element_type=F32) + jnp.dot(m, lo, preferred_element_type=F32)


def _step_rows(n, run, reverse):
    return pl.ds(pl.multiple_of((run - 1 - n if reverse else n) * SUBLANES, SUBLANES), SUBLANES)


def _s5_scan(bu_ref, a_ref, carry_ref, *, ts, reverse):
    run = ts // SUBLANES
    sub = lax.broadcasted_iota(jnp.int32, (SUBLANES, S5_COLS), 0)
    for c0 in range(0, B_STATES, S5_COLS):
        re = pl.ds(c0, S5_COLS)
        im = pl.ds(B_STATES + c0, S5_COLS)
        ar = jnp.broadcast_to(a_ref[0:1, re], (SUBLANES, S5_COLS))
        ai = jnp.broadcast_to(a_ref[0:1, im], (SUBLANES, S5_COLS))

        def sweep(xr, xi, store):
            def step(n, x):
                xr, xi = x
                rows = _step_rows(n, run, reverse)
                nr = ar * xr - ai * xi + bu_ref[rows, re]
                ni = ar * xi + ai * xr + bu_ref[rows, im]
                if store:
                    bu_ref[rows, re] = nr
                    bu_ref[rows, im] = ni
                return nr, ni
            return lax.fori_loop(0, run, step, (xr, xi), unroll=SCAN_UNROLL)

        zero = jnp.zeros((SUBLANES, S5_COLS), F32)
        er, ei = sweep(zero, zero, False)
        pr, pi = a_ref[1:2, re], a_ref[1:2, im]
        cr, ci = carry_ref[0:1, re], carry_ref[0:1, im]
        startr, starti = zero, zero
        for s_ in (range(SUBLANES - 1, -1, -1) if reverse else range(SUBLANES)):
            startr = jnp.where(sub == s_, cr, startr)
            starti = jnp.where(sub == s_, ci, starti)
            nr = pr * cr - pi * ci + er[s_:s_ + 1]
            ni = pr * ci + pi * cr + ei[s_:s_ + 1]
            cr, ci = nr, ni
        carry_ref[0:1, re] = cr
        carry_ref[0:1, im] = ci
        sweep(startr, starti, True)


def _s5_kernel(uf_ref, ub_ref, wb_ref, a_ref, wc_ref, yf_ref, yb_ref, carry_ref, bu_ref, *, ts):
    @pl.when(pl.program_id(1) == 0)
    def _():
        carry_ref[...] = jnp.zeros(carry_ref.shape, F32)

    to_runs = _run_major(ts, True)
    to_time = _run_major(ts, False)
    for di, (u_ref, y_ref) in enumerate(((uf_ref, yf_ref), (ub_ref, yb_ref))):
        u = jnp.dot(to_runs, u_ref[0].astype(BF16), preferred_element_type=F32)
        bu_ref[...] = _dot(u, wb_ref[di])
        _s5_scan(bu_ref, a_ref.at[di], carry_ref.at[di], ts=ts, reverse=(di == 1))
        y_ref[0] = _permute_rows(to_time, _dot(bu_ref[...], wc_ref[di]))


def _s5_tables(lam_re, lam_im, log_step, b_re, b_im, c_re, c_im, run):
    lr, li = lam_re.astype(F32), lam_im.astype(F32)
    dt = jnp.exp(log_step.astype(F32))[..., None]
    e = jnp.exp(lr * dt)
    abar_re, abar_im = e * jnp.cos(li * dt), e * jnp.sin(li * dt)
    den = lr * lr + li * li
    nr, ni = abar_re - 1.0, abar_im
    fr, fi = (nr * lr + ni * li) / den, (ni * lr - nr * li) / den
    br, bi = b_re.astype(F32), b_im.astype(F32)
    bbar_re = fr[..., None] * br - fi[..., None] * bi
    bbar_im = fr[..., None] * bi + fi[..., None] * br
    eye = jnp.eye(B_GROUPS, dtype=F32)
    wb = jnp.concatenate([jnp.einsum('dgpc,gh->dgchp', t, eye).reshape(2, B_WIDTH, B_STATES)
                          for t in (bbar_re, bbar_im)], axis=-1)
    wc = jnp.concatenate([jnp.einsum('dgcp,gh->dhpgc', t, eye).reshape(2, B_STATES, B_WIDTH)
                          for t in (c_re.astype(F32), -c_im.astype(F32))], axis=1)
    pr, pi = abar_re, abar_im
    for _ in range(run - 1):
        pr, pi = pr * abar_re - pi * abar_im, pr * abar_im + pi * abar_re
    a = jnp.stack([jnp.concatenate([abar_re.reshape(2, B_STATES), abar_im.reshape(2, B_STATES)], axis=-1),
                   jnp.concatenate([pr.reshape(2, B_STATES), pi.reshape(2, B_STATES)], axis=-1)], axis=1)
    return wb.astype(BF16), a, wc.astype(BF16)


def _s5(z, tables, *, ts):
    bt, s, _ = z.shape
    wb, a, wc = tables
    nt = s // ts
    ucol = 4 * A_WIDTH // B_WIDTH
    kernel = functools.partial(_s5_kernel, ts=ts)
    full = lambda shape: pl.BlockSpec(shape, lambda b, i: (0,) * len(shape))
    return pl.pallas_call(
        kernel,
        grid=(bt, nt),
        in_specs=[
            pl.BlockSpec((1, ts, B_WIDTH), lambda b, i: (b, i, ucol)),
            pl.BlockSpec((1, ts, B_WIDTH), lambda b, i: (b, nt - 1 - i, ucol)),
            full(wb.shape), full(a.shape), full(wc.shape),
        ],
        out_specs=[
            pl.BlockSpec((1, ts, B_WIDTH), lambda b, i: (b, i, 0)),
            pl.BlockSpec((1, ts, B_WIDTH), lambda b, i: (b, nt - 1 - i, 0)),
        ],
        out_shape=[jax.ShapeDtypeStruct((bt, s, B_WIDTH), F32)] * 2,
        scratch_shapes=[pltpu.VMEM((2, 1, 2 * B_STATES), F32), pltpu.VMEM((ts, 2 * B_STATES), F32)],
        compiler_params=_params("parallel", "arbitrary"),
    )(z, z, wb, a, wc)


def _gelu_tanh(x):
    return 0.5 * x * (1.0 + jnp.tanh(math.sqrt(2.0 / math.pi) * (x + 0.044715 * (x * x * x))))


def _silu(x):
    return x * jax.nn.sigmoid(x)


def _even_out_kernel(h_ref, ya_ref, yf_ref, yb_ref, u_ref, gb_ref, d_ref, gw_ref, gbias_ref, wa_ref, wb_ref,
                     gpost_ref, o_ref):
    y_s5 = _gelu_tanh(d_ref[...] * u_ref[0] + yf_ref[0] + yb_ref[0])
    y_glu = y_s5 * jax.nn.sigmoid(jnp.dot(y_s5.astype(BF16), gw_ref[...], preferred_element_type=F32)
                                  + gbias_ref[...])
    y_b = y_glu * _silu(gb_ref[0])
    y = (jnp.dot(ya_ref[0].astype(BF16), wa_ref[...], preferred_element_type=F32)
         + jnp.dot(y_b.astype(BF16), wb_ref[...], preferred_element_type=F32))
    o_ref[0] = h_ref[0] + _rms(y, gpost_ref[...])


def _even_out(h, y_a, y_f, y_b, z, d_skip, glu_w, glu_b, w_out, g_post, *, tm):
    bt, s, d = h.shape
    ucol = 4 * A_WIDTH // B_WIDTH
    row = lambda width, col=0: pl.BlockSpec((1, tm, width), lambda b, i: (b, i, col))
    full = lambda shape: pl.BlockSpec(shape, lambda b, i: (0,) * len(shape))
    return pl.pallas_call(
        _even_out_kernel,
        grid=(bt, s // tm),
        in_specs=[row(d), row(A_WIDTH), row(B_WIDTH), row(B_WIDTH), row(B_WIDTH, ucol), row(B_WIDTH, ucol + 1),
                  full((1, B_WIDTH)), full((B_WIDTH, B_WIDTH)), full((1, B_WIDTH)),
                  full((A_WIDTH, d)), full((B_WIDTH, d)), full((1, d))],
        out_specs=row(d),
        out_shape=jax.ShapeDtypeStruct((bt, s, d), F32),
        compiler_params=_params("parallel", "parallel"),
    )(h, y_a, y_f, y_b, z, z, d_skip.reshape(1, -1), glu_w, glu_b.reshape(1, -1),
      w_out[:A_WIDTH], w_out[A_WIDTH:], g_post.reshape(1, -1))


def _odd_out_kernel(h_ref, of_ref, ob_ref, gc_ref, hf_ref, hb_ref, gd_ref, onorm_ref, wc_ref, wd_ref, gpost_ref,
                    o_ref):
    o_c = of_ref[0] + ob_ref[0]
    heads = [o_c[:, hd * HEAD_DIM:(hd + 1) * HEAD_DIM] for hd in range(C_HEADS)]
    o_c = jnp.concatenate([t * lax.rsqrt(jnp.mean(t * t, axis=-1, keepdims=True) + EPS) for t in heads], axis=-1)
    y_c = o_c * onorm_ref[...] * _silu(gc_ref[0])
    y_d = (hf_ref[0] + hb_ref[0]) * _silu(gd_ref[0])
    y = (jnp.dot(y_c.astype(BF16), wc_ref[...], preferred_element_type=F32)
         + jnp.dot(y_d.astype(BF16), wd_ref[...], preferred_element_type=F32))
    o_ref[0] = h_ref[0] + _rms(y, gpost_ref[...])


def _odd_out(h, o_f, o_b, h_f, h_b, z, onorm, w_out, g_post, *, tm):
    bt, s, d = h.shape
    gccol = 4 * C_WIDTH // C_WIDTH
    gdcol = (5 * C_WIDTH + D_WIDTH) // D_WIDTH
    row = lambda width, col=0: pl.BlockSpec((1, tm, width), lambda b, i: (b, i, col))
    full = lambda shape: pl.BlockSpec(shape, lambda b, i: (0,) * len(shape))
    return pl.pallas_call(
        _odd_out_kernel,
        grid=(bt, s // tm),
        in_specs=[row(d), row(C_WIDTH), row(C_WIDTH), row(C_WIDTH, gccol),
                  row(D_WIDTH), row(D_WIDTH), row(D_WIDTH, gdcol),
                  full((1, C_WIDTH)), full((C_WIDTH, d)), full((D_WIDTH, d)), full((1, d))],
        out_specs=row(d),
        out_shape=jax.ShapeDtypeStruct((bt, s, d), F32),
        compiler_params=_params("parallel", "parallel"),
    )(h, o_f, o_b, z, h_f, h_b, z, onorm.reshape(1, -1), w_out[:C_WIDTH], w_out[C_WIDTH:],
      g_post.reshape(1, -1))


def _xattn_kernel(h_ref, gpre_ref, wq_ref, kv_ref, wo_ref, gpost_ref, o_ref):
    h = h_ref[0]
    q = _dot(_rms(h, gpre_ref[...]), wq_ref[...]) * (HEAD_DIM ** -0.5)
    heads = []
    for hd in range(X_HEADS):
        cols = slice(hd * HEAD_DIM, (hd + 1) * HEAD_DIM)
        k = kv_ref[0, :, cols]
        v = kv_ref[0, :, X_WIDTH + hd * HEAD_DIM:X_WIDTH + (hd + 1) * HEAD_DIM]
        sc = _dot_t(q[:, cols], k)
        p = jnp.exp(sc - jnp.max(sc, axis=-1, keepdims=True))
        heads.append(_dot(p, v) / jnp.sum(p, axis=-1, keepdims=True))
    o = jnp.concatenate(heads, axis=-1)
    o_ref[0] = h + _rms(_dot(o, wo_ref[...]), gpost_ref[...])


def _xattn(h, kv, g_pre, wq, wo, g_post, *, tm):
    bt, s, d = h.shape
    full = lambda shape: pl.BlockSpec(shape, lambda b, i: (0,) * len(shape))
    return pl.pallas_call(
        _xattn_kernel,
        grid=(bt, s // tm),
        in_specs=[pl.BlockSpec((1, tm, d), lambda b, i: (b, i, 0)), full((1, d)), full((d, X_WIDTH)),
                  pl.BlockSpec((1, N_MEM, 2 * X_WIDTH), lambda b, i: (b, 0, 0)),
                  full((X_WIDTH, d)), full((1, d))],
        out_specs=pl.BlockSpec((1, tm, d), lambda b, i: (b, i, 0)),
        out_shape=jax.ShapeDtypeStruct((bt, s, d), F32),
        compiler_params=_params("parallel", "parallel"),
    )(h, g_pre.reshape(1, -1), wq, kv, wo, g_post.reshape(1, -1))


def _hgrn_kernel(q_ref, f_ref, v_ref, lb_ref, o_ref, state_ref, *, th, reverse):
    @pl.when(pl.program_id(2) == 0)
    def _():
        state_ref[...] = jnp.zeros(state_ref.shape, F32)

    lb = lb_ref[...]
    fx = f_ref[0]
    qf = _silu(q_ref[0])
    v = v_ref[0]
    logf = jnp.log(lb + (1.0 - lb) * jax.nn.sigmoid(fx))
    kf = (1.0 - lb) * jax.nn.sigmoid(-fx)

    t = lax.broadcasted_iota(jnp.int32, (th, th), 0)
    r = lax.broadcasted_iota(jnp.int32, (th, th), 1)
    same = (t // C_SUB) == (r // C_SUB)
    causal = same & ((r >= t) if reverse else (r <= t))
    cum = _permute_rows(causal.astype(BF16), logf)
    nblk = th // C_SUB
    half = C_SUB // 2
    mid_row = [j * C_SUB + (half if reverse else half - 1) for j in range(nblk)]
    tot_row = [j * C_SUB + (0 if reverse else C_SUB - 1) for j in range(nblk)]
    width = cum.shape[1]
    spread = lambda rows: jnp.concatenate(
        [jnp.broadcast_to(cum[i:i + 1], (C_SUB, width)) for i in rows], axis=0)
    mid = spread(mid_row)
    tot = spread(tot_row)
    q_mid = qf * jnp.exp(cum - mid)
    k_mid = kf * jnp.exp(mid - cum)
    q_in = qf * jnp.exp(cum)
    k_out = kf * jnp.exp(tot - cum)

    heads = [slice(n * HEAD_DIM, (n + 1) * HEAD_DIM) for n in range(width // HEAD_DIM)]
    att = [jnp.where(causal, _dot_t(q_mid[:, hd], k_mid[:, hd]), 0.0) for hd in heads]
    intra = [_dot(a, v[:, hd]) for a, hd in zip(att, heads)]
    state = [state_ref[n] for n in range(len(heads))]
    inter = [[None] * nblk for _ in heads]
    for j in (range(nblk - 1, -1, -1) if reverse else range(nblk)):
        rows = slice(j * C_SUB, (j + 1) * C_SUB)
        for n, hd in enumerate(heads):
            inter[n][j] = _dot_t(q_in[rows, hd], state[n])
            state[n] = (state[n] * jnp.exp(cum[tot_row[j]:tot_row[j] + 1, hd])
                        + _tdot(v[rows, hd], k_out[rows, hd]))
    for n in range(len(heads)):
        state_ref[n] = state[n]
    o_ref[0] = jnp.concatenate([a + jnp.concatenate(b, axis=0) for a, b in zip(intra, inter)], axis=1)


def _hgrn(z, lb, *, th, reverse):
    bt, s, _ = z.shape
    nt = s // th
    groups = C_HEADS // C_HEADS_PER_STEP
    width = C_HEADS_PER_STEP * HEAD_DIM
    fcol = 2 * groups if reverse else groups
    tile = (lambda i: nt - 1 - i) if reverse else (lambda i: i)
    col = lambda c0: pl.BlockSpec((1, th, width), lambda b, h, i: (b, tile(i), c0 + h))
    kernel = functools.partial(_hgrn_kernel, th=th, reverse=reverse)
    return pl.pallas_call(
        kernel,
        grid=(bt, groups, nt),
        in_specs=[col(0), col(fcol), col(3 * groups), pl.BlockSpec((1, width), lambda b, h, i: (0, h))],
        out_specs=col(0),
        out_shape=jax.ShapeDtypeStruct((bt, s, C_WIDTH), F32),
        scratch_shapes=[pltpu.VMEM((C_HEADS_PER_STEP, HEAD_DIM, HEAD_DIM), F32)],
        compiler_params=_params("parallel", "parallel", "arbitrary"),
    )(z, z, z, lb.reshape(1, C_WIDTH))


D_HALO = SUBLANES
LRU_COLS = 512


def _lru_scan(a_ref, bx_ref, carry_ref, *, tl, reverse):
    run = tl // SUBLANES
    sub = lax.broadcasted_iota(jnp.int32, (SUBLANES, LRU_COLS), 0)
    for c0 in range(0, D_WIDTH, LRU_COLS):
        cols = pl.ds(c0, LRU_COLS)

        def first(n, x):
            hcur, prod = x
            rows = _step_rows(n, run, reverse)
            a = a_ref[rows, cols]
            return a * hcur + bx_ref[rows, cols], a * prod

        def second(n, hcur):
            rows = _step_rows(n, run, reverse)
            hcur = a_ref[rows, cols] * hcur + bx_ref[rows, cols]
            bx_ref[rows, cols] = hcur
            return hcur

        zero = jnp.zeros((SUBLANES, LRU_COLS), F32)
        hcur, prod = lax.fori_loop(0, run, first, (zero, zero + 1.0), unroll=SCAN_UNROLL)
        c = carry_ref[0:1, cols]
        start = zero
        for s_ in (range(SUBLANES - 1, -1, -1) if reverse else range(SUBLANES)):
            start = jnp.where(sub == s_, c, start)
            c = prod[s_:s_ + 1] * c + hcur[s_:s_ + 1]
        carry_ref[0:1, cols] = c
        lax.fori_loop(0, run, second, start, unroll=SCAN_UNROLL)


def _lru_kernel(xf_ref, xfp_ref, xfn_ref, xb_ref, xbp_ref, xbn_ref, cw_ref, cb_ref, w_ref, bias_ref, sp_ref,
                hf_ref, hb_ref, carry_ref, a_ref, bx_ref, *, tl, nt):
    i = pl.program_id(1)
    to_runs = _run_major(tl, True)
    to_time = _run_major(tl, False)

    @pl.when(i == 0)
    def _():
        carry_ref[...] = jnp.zeros(carry_ref.shape, F32)

    dirs = ((xf_ref, xfp_ref, xfn_ref, hf_ref, i), (xb_ref, xbp_ref, xbn_ref, hb_ref, nt - 1 - i))
    for di, (x_ref, xp_ref, xn_ref, h_ref, tile) in enumerate(dirs):
        x = x_ref[0]
        prev = jnp.where(tile > 0, xp_ref[0], 0.0)
        nxt = jnp.where(tile < nt - 1, xn_ref[0], 0.0)
        ext = jnp.concatenate([prev, x, nxt], axis=0)
        xc = cb_ref[...]
        for tap in range(4):
            xc = xc + cw_ref[tap:tap + 1, :] * ext[D_HALO - 1 + tap:D_HALO - 1 + tap + tl]
        xc = _permute_rows(to_runs, xc)
        gates = []
        for n in range(D_BLOCKS):
            cols = slice(n * D_BLOCK_DIM, (n + 1) * D_BLOCK_DIM)
            gates.append(_dot(xc[:, cols], w_ref[di, n]))
        r = jax.nn.sigmoid(jnp.concatenate([g[:, :D_BLOCK_DIM] for g in gates], axis=-1) + bias_ref[di, 0:1])
        ig = jax.nn.sigmoid(jnp.concatenate([g[:, D_BLOCK_DIM:] for g in gates], axis=-1) + bias_ref[di, 1:2])
        log_a = -LRU_C * r * sp_ref[di:di + 1]
        a = jnp.exp(log_a)
        a_ref[...] = a
        bx_ref[...] = jnp.sqrt(-jnp.tanh(log_a) * (a * a + 1.0)) * (ig * xc)
        _lru_scan(a_ref, bx_ref, carry_ref.at[di], tl=tl, reverse=(di == 1))
        h_ref[0] = _permute_rows(to_time, bx_ref[...])


def _rglru(z, conv_w, conv_b, wr, br, wi, bi, lam, *, tl):
    bt, s, _ = z.shape
    nt = s // tl
    xcol = 5 * C_WIDTH // D_WIDTH
    hpt = tl // D_HALO
    nh = s // D_HALO
    w = jnp.concatenate([wr, wi], axis=-1).astype(BF16)
    bias = jnp.stack([br, bi], axis=1).astype(F32)
    sp = jax.nn.softplus(-lam.astype(F32))
    fwd = lambda i: i
    bwd = lambda i: nt - 1 - i
    specs = []
    for tile in (fwd, bwd):
        specs += [
            pl.BlockSpec((1, tl, D_WIDTH), lambda b, i, tile=tile: (b, tile(i), xcol)),
            pl.BlockSpec((1, D_HALO, D_WIDTH),
                         lambda b, i, tile=tile: (b, jnp.maximum(tile(i) * hpt - 1, 0), xcol)),
            pl.BlockSpec((1, D_HALO, D_WIDTH),
                         lambda b, i, tile=tile: (b, jnp.minimum((tile(i) + 1) * hpt, nh - 1), xcol)),
        ]
    full = lambda shape: pl.BlockSpec(shape, lambda b, i: (0,) * len(shape))
    kernel = functools.partial(_lru_kernel, tl=tl, nt=nt)
    return pl.pallas_call(
        kernel,
        grid=(bt, nt),
        in_specs=specs + [full((4, D_WIDTH)), full((1, D_WIDTH)), full(w.shape), full(bias.shape), full(sp.shape)],
        out_specs=[pl.BlockSpec((1, tl, D_WIDTH), lambda b, i: (b, i, 0)),
                   pl.BlockSpec((1, tl, D_WIDTH), lambda b, i: (b, nt - 1 - i, 0))],
        out_shape=[jax.ShapeDtypeStruct((bt, s, D_WIDTH), F32)] * 2,
        scratch_shapes=[pltpu.VMEM((2, 1, D_WIDTH), F32), pltpu.VMEM((tl, D_WIDTH), F32),
                        pltpu.VMEM((tl, D_WIDTH), F32)],
        compiler_params=_params("parallel", "arbitrary"),
    )(z, z, z, z, z, z, conv_w.astype(F32), conv_b.reshape(1, -1).astype(F32), w, bias, sp)


TM_PROJ = 512
TN_PROJ = 1024
TQ_ATTN = 1024
TS_S5 = 256
TM_OUT = 256
TH_HGRN = 256
TL_LRU = 256


def _trunk(x, mem, p):
    bf = lambda t: t.astype(BF16)
    lb_all = jnp.cumsum(jax.nn.softmax(p['hgrn_lb_logits'].astype(F32), axis=0), axis=0)
    lb_all = lb_all - lb_all[0:1]
    slopes = jnp.asarray(2.0 ** (-8.0 * np.arange(1, A_HEADS + 1) / A_HEADS), dtype=F32)
    depth = p['norm_mix_pre'].shape[0]
    h = x
    for l in range(depth):
        if l % 2 == 0:
            e = l // 2
            z = _norm_matmul(h, p['norm_mix_pre'][l], bf(p['ev_w_in'][e]), tm=TM_PROJ, tn=TN_PROJ)
            y_a = _dilated_attention(z, slopes, tq=TQ_ATTN)
            tables = _s5_tables(p['s5_lam_re'][e], p['s5_lam_im'][e], p['s5_log_step'][e], p['s5_b_re'][e],
                                p['s5_b_im'][e], p['s5_c_re'][e], p['s5_c_im'][e], TS_S5 // SUBLANES)
            y_f, y_b = _s5(z, tables, ts=TS_S5)
            h = _even_out(h, y_a, y_f, y_b, z, p['s5_d'][e].astype(F32), bf(p['s5_glu_w'][e]),
                          p['s5_glu_b'][e].astype(F32), bf(p['ev_w_out'][e]), p['norm_mix_post'][l], tm=TM_OUT)
        else:
            o = l // 2
            z = _norm_matmul(h, p['norm_mix_pre'][l], bf(p['od_w_in'][o]), tm=TM_PROJ, tn=TN_PROJ)
            o_f = _hgrn(z, lb_all[l], th=TH_HGRN, reverse=False)
            o_b = _hgrn(z, lb_all[l], th=TH_HGRN, reverse=True)
            h_f, h_b = _rglru(z, p['lru_conv_w'][o], p['lru_conv_b'][o], p['lru_wr'][o], p['lru_br'][o],
                              p['lru_wi'][o], p['lru_bi'][o], p['lru_lambda'][o], tl=TL_LRU)
            h = _odd_out(h, o_f, o_b, h_f, h_b, z, p['hgrn_onorm'][o].astype(F32), bf(p['od_w_out'][o]),
                         p['norm_mix_post'][l], tm=TM_OUT)
        kv = _norm_matmul(mem, p['x_mem_norm'][l], bf(p['x_wkv'][l]), tm=N_MEM, tn=2 * X_WIDTH)
        h = _xattn(h, kv, p['norm_x_pre'][l], bf(p['x_wq'][l]), bf(p['x_wo'][l]), p['norm_x_post'][l], tm=TM_OUT)
    return h


def kernel(x_prompt, x_sample, mem_prompt, mem_sample, norm_mix_pre, norm_mix_post, norm_x_pre, norm_x_post, ev_w_in, ev_w_out, s5_lam_re, s5_lam_im, s5_log_step, s5_b_re, s5_b_im, s5_c_re, s5_c_im, s5_d, s5_glu_w, s5_glu_b, od_w_in, od_w_out, hgrn_lb_logits, hgrn_onorm, lru_conv_w, lru_conv_b, lru_wr, lru_br, lru_wi, lru_bi, lru_lambda, x_wq, x_wkv, x_wo, x_mem_norm):
    params = dict(norm_mix_pre=norm_mix_pre, norm_mix_post=norm_mix_post, norm_x_pre=norm_x_pre,
                  norm_x_post=norm_x_post, ev_w_in=ev_w_in, ev_w_out=ev_w_out, s5_lam_re=s5_lam_re,
                  s5_lam_im=s5_lam_im, s5_log_step=s5_log_step, s5_b_re=s5_b_re, s5_b_im=s5_b_im,
                  s5_c_re=s5_c_re, s5_c_im=s5_c_im, s5_d=s5_d, s5_glu_w=s5_glu_w, s5_glu_b=s5_glu_b,
                  od_w_in=od_w_in, od_w_out=od_w_out, hgrn_lb_logits=hgrn_lb_logits, hgrn_onorm=hgrn_onorm,
                  lru_conv_w=lru_conv_w, lru_conv_b=lru_conv_b, lru_wr=lru_wr, lru_br=lru_br,
                  lru_wi=lru_wi, lru_bi=lru_bi, lru_lambda=lru_lambda, x_wq=x_wq, x_wkv=x_wkv,
                  x_wo=x_wo, x_mem_norm=x_mem_norm)
    return (_trunk(x_prompt, mem_prompt, params), _trunk(x_sample, mem_sample, params))
```

```python
import functools
import math

import jax
import jax.numpy as jnp
import numpy as np
from jax import lax
from jax.experimental import pallas as pl
from jax.experimental.pallas import tpu as pltpu

F32 = jnp.float32
BF16 = jnp.bfloat16

D_MODEL = 2048
N_MEM = 256
EPS = 1e-6
NEG_INF = -1e30

A_HEADS = 12
HEAD_DIM = 128
A_WIDTH = A_HEADS * HEAD_DIM
A_DILATIONS = (1, 4, 16)
A_RADIUS = 64
B_WIDTH = 512
B_GROUP = 16
B_GROUPS = B_WIDTH // B_GROUP
B_STATE = 64
B_STATES = B_GROUPS * B_STATE
C_HEADS = 8
C_WIDTH = C_HEADS * HEAD_DIM
C_SUB = 32
C_HEADS_PER_STEP = 8
D_WIDTH = 1024
D_BLOCKS = 8
D_BLOCK_DIM = D_WIDTH // D_BLOCKS
LRU_C = 8.0
X_HEADS = 4
X_WIDTH = X_HEADS * HEAD_DIM

EVEN_IN = 4 * A_WIDTH + 2 * B_WIDTH
ODD_IN = 5 * C_WIDTH + 2 * D_WIDTH

SUBLANES = 8
LANES = 128
VMEM_LIMIT = 56 * 1024 * 1024


def _params(*sem):
    return pltpu.CompilerParams(dimension_semantics=sem, vmem_limit_bytes=VMEM_LIMIT)


def _rms(x, g):
    return x * lax.rsqrt(jnp.mean(x * x, axis=-1, keepdims=True) + EPS) * g


def _dot(a, b):
    return jnp.dot(a.astype(BF16), b.astype(BF16), preferred_element_type=F32)


def _dot_t(a, b):
    return lax.dot_general(a.astype(BF16), b.astype(BF16), (((1,), (1,)), ((), ())),
                           preferred_element_type=F32)


def _tdot(a, b):
    return lax.dot_general(a.astype(BF16), b.astype(BF16), (((0,), (0,)), ((), ())),
                           preferred_element_type=F32)


def _norm_matmul_kernel(x_ref, g_ref, w_ref, o_ref, xn_ref):
    @pl.when(pl.program_id(2) == 0)
    def _():
        xn_ref[...] = _rms(x_ref[0], g_ref[...]).astype(BF16)

    o_ref[0] = jnp.dot(xn_ref[...], w_ref[...], preferred_element_type=F32).astype(o_ref.dtype)


def _norm_matmul(x, g, w, *, tm, tn, out_dtype=F32):
    bt, s, d = x.shape
    n = w.shape[1]
    return pl.pallas_call(
        _norm_matmul_kernel,
        grid=(bt, s // tm, n // tn),
        in_specs=[
            pl.BlockSpec((1, tm, d), lambda b, i, j: (b, i, 0)),
            pl.BlockSpec((1, d), lambda b, i, j: (0, 0)),
            pl.BlockSpec((d, tn), lambda b, i, j: (0, j)),
        ],
        out_specs=pl.BlockSpec((1, tm, tn), lambda b, i, j: (b, i, j)),
        out_shape=jax.ShapeDtypeStruct((bt, s, n), out_dtype),
        scratch_shapes=[pltpu.VMEM((tm, d), BF16)],
        compiler_params=_params("parallel", "parallel", "arbitrary"),
    )(x, g.reshape(1, d), w)


A_QB = 64
A_KW = 3 * A_QB
A_UNROLL = 8
A_SHIFTS = (0, -A_RADIUS, -2 * A_RADIUS)


def _attn_kernel(slope_ref, q_ref, k_ref, v_ref, g_ref, o_ref, bias_ref, *stat_refs, tq, seq):
    t0 = pl.program_id(2) * tq
    stats = [stat_refs[3 * n:3 * n + 3] for n in range(len(A_DILATIONS))]

    @pl.when(pl.program_id(2) == 0)
    def _():
        slope = slope_ref[0]
        drel = (lax.broadcasted_iota(jnp.int32, (A_QB, A_KW), 1)
                - lax.broadcasted_iota(jnp.int32, (A_QB, A_KW), 0))
        for n, dil in enumerate(A_DILATIONS):
            for j, shift in enumerate(A_SHIFTS):
                dist = jnp.abs(drel + shift)
                bias_ref[n * len(A_SHIFTS) + j] = jnp.where(
                    dist <= A_RADIUS, -(slope * (dil * dist).astype(F32)), NEG_INF)

    for n, (dil, (m_ref, l_ref, acc_ref)) in enumerate(zip(A_DILATIONS, stats)):
        sub_len = seq // dil
        blocks = tq // (dil * A_QB)

        def steps(units, n=n, dil=dil, sub_len=sub_len, m_ref=m_ref, l_ref=l_ref, acc_ref=acc_ref):
            staged = []
            for r, c in units:
                qj = t0 // dil + c * A_QB
                kj = jnp.clip(qj - A_RADIUS, 0, sub_len - A_KW)
                rows = pl.ds(r + pl.multiple_of(c * (A_QB * dil), A_QB), A_QB, stride=dil)
                krows = pl.ds(r + pl.multiple_of(kj * dil, A_QB), A_KW, stride=dil)
                sc = _dot_t(q_ref[rows, :] * (HEAD_DIM ** -0.5), k_ref[krows, :])
                staged.append((rows, krows, sc + bias_ref[n * len(A_SHIFTS) + (qj - kj) // A_RADIUS]))
            probs = []
            for rows, krows, sc in staged:
                m = jnp.max(sc, axis=-1, keepdims=True)
                p = jnp.exp(sc - m)
                probs.append((m, jnp.sum(p, axis=-1, keepdims=True), _dot(p, v_ref[krows, :])))
            for (rows, _, _), (m, l, acc) in zip(staged, probs):
                m_ref[rows, :] = jnp.broadcast_to(m, (A_QB, HEAD_DIM))
                l_ref[rows, :] = jnp.broadcast_to(l, (A_QB, HEAD_DIM))
                acc_ref[rows, :] = acc

        if dil <= A_UNROLL:
            per = A_UNROLL // dil

            def group_step(gi, carry, dil=dil, per=per, steps=steps):
                steps([(r, gi * per + cc) for cc in range(per) for r in range(dil)])
                return carry

            lax.fori_loop(0, blocks // per, group_step, 0)
        else:
            def block_step(c, carry, dil=dil, steps=steps):
                for r0 in range(0, dil, A_UNROLL):
                    steps([(r0 + u, c) for u in range(A_UNROLL)])
                return carry

            lax.fori_loop(jnp.int32(0), jnp.int32(blocks), block_step, 0)

    m_max = functools.reduce(jnp.maximum, [m_ref[...] for m_ref, _, _ in stats])
    num = 0.0
    den = 0.0
    for m_ref, l_ref, acc_ref in stats:
        w = jnp.exp(m_ref[...] - m_max)
        num = num + w * acc_ref[...]
        den = den + w * l_ref[...]
    o_ref[...] = (num / den * _silu(g_ref[...])).astype(o_ref.dtype)


def _dilated_attention(z, slopes, *, tq):
    bt, s, _ = z.shape
    assert s % tq == 0 and tq % (A_DILATIONS[-1] * A_QB) == 0 and s // A_DILATIONS[-1] >= A_KW
    kernel = functools.partial(_attn_kernel, tq=tq, seq=s)
    return pl.pallas_call(
        kernel,
        grid=(bt, A_HEADS, s // tq),
        in_specs=[
            pl.BlockSpec((1, 1, 1), lambda b, h, i: (h, 0, 0)),
            pl.BlockSpec((None, tq, HEAD_DIM), lambda b, h, i: (b, i, h)),
            pl.BlockSpec((None, s, HEAD_DIM), lambda b, h, i: (b, 0, A_HEADS + h)),
            pl.BlockSpec((None, s, HEAD_DIM), lambda b, h, i: (b, 0, 2 * A_HEADS + h)),
            pl.BlockSpec((None, tq, HEAD_DIM), lambda b, h, i: (b, i, 3 * A_HEADS + h)),
        ],
        out_specs=pl.BlockSpec((None, tq, HEAD_DIM), lambda b, h, i: (b, i, h)),
        out_shape=jax.ShapeDtypeStruct((bt, s, A_WIDTH), F32),
        scratch_shapes=([pltpu.VMEM((len(A_DILATIONS) * len(A_SHIFTS), A_QB, A_KW), F32)]
                        + [pltpu.VMEM((tq, HEAD_DIM), F32)] * (3 * len(A_DILATIONS))),
        compiler_params=_params("parallel", "parallel", "arbitrary"),
    )(slopes.reshape(A_HEADS, 1, 1), z, z, z, z)


S5_COLS = 512
S5_KB = 256
SCAN_UNROLL = 4


def _run_major(n, from_time):
    a = lax.broadcasted_iota(jnp.int32, (n, n), 0)
    b = lax.broadcasted_iota(jnp.int32, (n, n), 1)
    pos, t = (a, b) if from_time else (b, a)
    return (t == (pos % SUBLANES) * (n // SUBLANES) + pos // SUBLANES).astype(BF16)


def _permute_rows(m, x):
    hi = x.astype(BF16)
    lo = (x - hi.astype(F32)).astype(BF16)
    return jnp.dot(m, hi, preferred_element_type=F32) + jnp.dot(m, lo, preferred_element_type=F32)


def _step_rows(n, run, reverse):
    return pl.ds(pl.multiple_of((run - 1 - n if reverse else n) * SUBLANES, SUBLANES), SUBLANES)


def _s5_scan(bu_ref, a_ref, carry_ref, *, ts, reverse):
    run = ts // SUBLANES
    sub = lax.broadcasted_iota(jnp.int32, (SUBLANES, S5_COLS), 0)
    for c0 in range(0, B_STATES, S5_COLS):
        re = pl.ds(c0, S5_COLS)
        im = pl.ds(B_STATES + c0, S5_COLS)
        ar = jnp.broadcast_to(a_ref[0:1, re], (SUBLANES, S5_COLS))
        ai = jnp.broadcast_to(a_ref[0:1, im], (SUBLANES, S5_COLS))

        def sweep(xr, xi, store):
            def step(n, x):
                xr, xi = x
                rows = _step_rows(n, run, reverse)
                nr = ar * xr - ai * xi + bu_ref[rows, re]
                ni = ar * xi + ai * xr + bu_ref[rows, im]
                if store:
                    bu_ref[rows, re] = nr
                    bu_ref[rows, im] = ni
                return nr, ni
            return lax.fori_loop(0, run, step, (xr, xi), unroll=SCAN_UNROLL)

        zero = jnp.zeros((SUBLANES, S5_COLS), F32)
        er, ei = sweep(zero, zero, False)
        pr, pi = a_ref[1:2, re], a_ref[1:2, im]
        cr, ci = carry_ref[0:1, re], carry_ref[0:1, im]
        startr, starti = zero, zero
        for s_ in (range(SUBLANES - 1, -1, -1) if reverse else range(SUBLANES)):
            startr = jnp.where(sub == s_, cr, startr)
            starti = jnp.where(sub == s_, ci, starti)
            nr = pr * cr - pi * ci + er[s_:s_ + 1]
            ni = pr * ci + pi * cr + ei[s_:s_ + 1]
            cr, ci = nr, ni
        carry_ref[0:1, re] = cr
        carry_ref[0:1, im] = ci
        sweep(startr, starti, True)


def _s5_kernel(uf_ref, ub_ref, wb_ref, a_ref, wc_ref, yf_ref, yb_ref, carry_ref, bu_ref, *, ts):
    @pl.when(pl.program_id(1) == 0)
    def _():
        carry_ref[...] = jnp.zeros(carry_ref.shape, F32)

    to_runs = _run_major(ts, True)
    to_time = _run_major(ts, False)
    span = B_STATES * S5_KB // B_WIDTH
    blocks = [(slice(k0, k0 + S5_KB), [slice(part + k0 * span // S5_KB, part + k0 * span // S5_KB + span)
                                       for part in (0, B_STATES)])
              for k0 in range(0, B_WIDTH, S5_KB)]
    for di, (u_ref, y_ref) in enumerate(((uf_ref, yf_ref), (ub_ref, yb_ref))):
        u = jnp.dot(to_runs, u_ref[0].astype(BF16), preferred_element_type=F32).astype(BF16)
        for ucols, parts in blocks:
            for scols in parts:
                bu_ref[:, scols] = jnp.dot(u[:, ucols], wb_ref[di, ucols, scols], preferred_element_type=F32)
        _s5_scan(bu_ref, a_ref.at[di], carry_ref.at[di], ts=ts, reverse=(di == 1))
        y = [sum(_dot(bu_ref[:, scols], wc_ref[di, scols, ucols]) for scols in parts) for ucols, parts in blocks]
        y_ref[0] = _permute_rows(to_time, jnp.concatenate(y, axis=1))


def _s5_tables(lam_re, lam_im, log_step, b_re, b_im, c_re, c_im, run):
    lr, li = lam_re.astype(F32), lam_im.astype(F32)
    dt = jnp.exp(log_step.astype(F32))[..., None]
    e = jnp.exp(lr * dt)
    abar_re, abar_im = e * jnp.cos(li * dt), e * jnp.sin(li * dt)
    den = lr * lr + li * li
    nr, ni = abar_re - 1.0, abar_im
    fr, fi = (nr * lr + ni * li) / den, (ni * lr - nr * li) / den
    br, bi = b_re.astype(F32), b_im.astype(F32)
    bbar_re = fr[..., None] * br - fi[..., None] * bi
    bbar_im = fr[..., None] * bi + fi[..., None] * br
    eye = jnp.eye(B_GROUPS, dtype=F32)
    wb = jnp.concatenate([jnp.einsum('dgpc,gh->dgchp', t, eye).reshape(2, B_WIDTH, B_STATES)
                          for t in (bbar_re, bbar_im)], axis=-1)
    wc = jnp.concatenate([jnp.einsum('dgcp,gh->dhpgc', t, eye).reshape(2, B_STATES, B_WIDTH)
                          for t in (c_re.astype(F32), -c_im.astype(F32))], axis=1)
    pr, pi = abar_re, abar_im
    for _ in range(run - 1):
        pr, pi = pr * abar_re - pi * abar_im, pr * abar_im + pi * abar_re
    a = jnp.stack([jnp.concatenate([abar_re.reshape(2, B_STATES), abar_im.reshape(2, B_STATES)], axis=-1),
                   jnp.concatenate([pr.reshape(2, B_STATES), pi.reshape(2, B_STATES)], axis=-1)], axis=1)
    return wb.astype(BF16), a, wc.astype(BF16)


def _s5(z, tables, *, ts):
    bt, s, _ = z.shape
    wb, a, wc = tables
    nt = s // ts
    ucol = 4 * A_WIDTH // B_WIDTH
    kernel = functools.partial(_s5_kernel, ts=ts)
    full = lambda shape: pl.BlockSpec(shape, lambda b, i: (0,) * len(shape))
    return pl.pallas_call(
        kernel,
        grid=(bt, nt),
        in_specs=[
            pl.BlockSpec((1, ts, B_WIDTH), lambda b, i: (b, i, ucol)),
            pl.BlockSpec((1, ts, B_WIDTH), lambda b, i: (b, nt - 1 - i, ucol)),
            full(wb.shape), full(a.shape), full(wc.shape),
        ],
        out_specs=[
            pl.BlockSpec((1, ts, B_WIDTH), lambda b, i: (b, i, 0)),
            pl.BlockSpec((1, ts, B_WIDTH), lambda b, i: (b, nt - 1 - i, 0)),
        ],
        out_shape=[jax.ShapeDtypeStruct((bt, s, B_WIDTH), F32)] * 2,
        scratch_shapes=[pltpu.VMEM((2, 1, 2 * B_STATES), F32), pltpu.VMEM((ts, 2 * B_STATES), F32)],
        compiler_params=_params("parallel", "arbitrary"),
    )(z, z, wb, a, wc)


def _gelu_tanh(x):
    return 0.5 * x * (1.0 + jnp.tanh(math.sqrt(2.0 / math.pi) * (x + 0.044715 * (x * x * x))))


def _silu(x):
    return x * jax.nn.sigmoid(x)


def _even_out_kernel(h_ref, ya_ref, yf_ref, yb_ref, u_ref, gb_ref, d_ref, gw_ref, gbias_ref, wa_ref, wb_ref,
                     gpost_ref, o_ref):
    y_s5 = _gelu_tanh(d_ref[...] * u_ref[0] + yf_ref[0] + yb_ref[0])
    y_glu = y_s5 * jax.nn.sigmoid(jnp.dot(y_s5.astype(BF16), gw_ref[...], preferred_element_type=F32)
                                  + gbias_ref[...])
    y_b = y_glu * _silu(gb_ref[0])
    y = (jnp.dot(ya_ref[0].astype(BF16), wa_ref[...], preferred_element_type=F32)
         + jnp.dot(y_b.astype(BF16), wb_ref[...], preferred_element_type=F32))
    o_ref[0] = h_ref[0] + _rms(y, gpost_ref[...])


def _even_out(h, y_a, y_f, y_b, z, d_skip, glu_w, glu_b, w_out, g_post, *, tm):
    bt, s, d = h.shape
    ucol = 4 * A_WIDTH // B_WIDTH
    row = lambda width, col=0: pl.BlockSpec((1, tm, width), lambda b, i: (b, i, col))
    full = lambda shape: pl.BlockSpec(shape, lambda b, i: (0,) * len(shape))
    return pl.pallas_call(
        _even_out_kernel,
        grid=(bt, s // tm),
        in_specs=[row(d), row(A_WIDTH), row(B_WIDTH), row(B_WIDTH), row(B_WIDTH, ucol), row(B_WIDTH, ucol + 1),
                  full((1, B_WIDTH)), full((B_WIDTH, B_WIDTH)), full((1, B_WIDTH)),
                  full((A_WIDTH, d)), full((B_WIDTH, d)), full((1, d))],
        out_specs=row(d),
        out_shape=jax.ShapeDtypeStruct((bt, s, d), F32),
        compiler_params=_params("parallel", "parallel"),
    )(h, y_a, y_f, y_b, z, z, d_skip.reshape(1, -1), glu_w, glu_b.reshape(1, -1),
      w_out[:A_WIDTH], w_out[A_WIDTH:], g_post.reshape(1, -1))


def _odd_out_kernel(h_ref, of_ref, ob_ref, gc_ref, hf_ref, hb_ref, gd_ref, onorm_ref, wc_ref, wd_ref, gpost_ref,
                    o_ref):
    o_c = of_ref[0] + ob_ref[0]
    heads = [o_c[:, hd * HEAD_DIM:(hd + 1) * HEAD_DIM] for hd in range(C_HEADS)]
    o_c = jnp.concatenate([t * lax.rsqrt(jnp.mean(t * t, axis=-1, keepdims=True) + EPS) for t in heads], axis=-1)
    y_c = o_c * onorm_ref[...] * _silu(gc_ref[0])
    y_d = (hf_ref[0] + hb_ref[0]) * _silu(gd_ref[0])
    y = (jnp.dot(y_c.astype(BF16), wc_ref[...], preferred_element_type=F32)
         + jnp.dot(y_d.astype(BF16), wd_ref[...], preferred_element_type=F32))
    o_ref[0] = h_ref[0] + _rms(y, gpost_ref[...])


def _odd_out(h, o_f, o_b, h_f, h_b, z, onorm, w_out, g_post, *, tm):
    bt, s, d = h.shape
    gccol = 4 * C_WIDTH // C_WIDTH
    gdcol = (5 * C_WIDTH + D_WIDTH) // D_WIDTH
    row = lambda width, col=0: pl.BlockSpec((1, tm, width), lambda b, i: (b, i, col))
    full = lambda shape: pl.BlockSpec(shape, lambda b, i: (0,) * len(shape))
    return pl.pallas_call(
        _odd_out_kernel,
        grid=(bt, s // tm),
        in_specs=[row(d), row(C_WIDTH), row(C_WIDTH), row(C_WIDTH, gccol),
                  row(D_WIDTH), row(D_WIDTH), row(D_WIDTH, gdcol),
                  full((1, C_WIDTH)), full((C_WIDTH, d)), full((D_WIDTH, d)), full((1, d))],
        out_specs=row(d),
        out_shape=jax.ShapeDtypeStruct((bt, s, d), F32),
        compiler_params=_params("parallel", "parallel"),
    )(h, o_f, o_b, z, h_f, h_b, z, onorm.reshape(1, -1), w_out[:C_WIDTH], w_out[C_WIDTH:],
      g_post.reshape(1, -1))


def _xattn_kernel(h_ref, gpre_ref, wq_ref, kv_ref, wo_ref, gpost_ref, o_ref):
    h = h_ref[0]
    q = _dot(_rms(h, gpre_ref[...]), wq_ref[...]) * (HEAD_DIM ** -0.5)
    heads = []
    for hd in range(X_HEADS):
        cols = slice(hd * HEAD_DIM, (hd + 1) * HEAD_DIM)
        k = kv_ref[0, :, cols]
        v = kv_ref[0, :, X_WIDTH + hd * HEAD_DIM:X_WIDTH + (hd + 1) * HEAD_DIM]
        sc = _dot_t(q[:, cols], k)
        p = jnp.exp(sc - jnp.max(sc, axis=-1, keepdims=True))
        heads.append(_dot(p, v) / jnp.sum(p, axis=-1, keepdims=True))
    o = jnp.concatenate(heads, axis=-1)
    o_ref[0] = h + _rms(_dot(o, wo_ref[...]), gpost_ref[...])


def _xattn(h, kv, g_pre, wq, wo, g_post, *, tm):
    bt, s, d = h.shape
    full = lambda shape: pl.BlockSpec(shape, lambda b, i: (0,) * len(shape))
    return pl.pallas_call(
        _xattn_kernel,
        grid=(bt, s // tm),
        in_specs=[pl.BlockSpec((1, tm, d), lambda b, i: (b, i, 0)), full((1, d)), full((d, X_WIDTH)),
                  pl.BlockSpec((1, N_MEM, 2 * X_WIDTH), lambda b, i: (b, 0, 0)),
                  full((X_WIDTH, d)), full((1, d))],
        out_specs=pl.BlockSpec((1, tm, d), lambda b, i: (b, i, 0)),
        out_shape=jax.ShapeDtypeStruct((bt, s, d), F32),
        compiler_params=_params("parallel", "parallel"),
    )(h, g_pre.reshape(1, -1), wq, kv, wo, g_post.reshape(1, -1))


def _hgrn_kernel(q_ref, f_ref, v_ref, lb_ref, o_ref, state_ref, *, th, reverse):
    @pl.when(pl.program_id(2) == 0)
    def _():
        state_ref[...] = jnp.zeros(state_ref.shape, F32)

    lb = lb_ref[...]
    fx = f_ref[0]
    qf = _silu(q_ref[0])
    v = v_ref[0]
    logf = jnp.log(lb + (1.0 - lb) * jax.nn.sigmoid(fx))
    kf = (1.0 - lb) * jax.nn.sigmoid(-fx)

    t = lax.broadcasted_iota(jnp.int32, (th, th), 0)
    r = lax.broadcasted_iota(jnp.int32, (th, th), 1)
    same = (t // C_SUB) == (r // C_SUB)
    causal = same & ((r >= t) if reverse else (r <= t))
    cum = _permute_rows(causal.astype(BF16), logf)
    nblk = th // C_SUB
    half = C_SUB // 2
    mid_row = [j * C_SUB + (half if reverse else half - 1) for j in range(nblk)]
    tot_row = [j * C_SUB + (0 if reverse else C_SUB - 1) for j in range(nblk)]
    width = cum.shape[1]
    spread = lambda rows: jnp.concatenate(
        [jnp.broadcast_to(cum[i:i + 1], (C_SUB, width)) for i in rows], axis=0)
    mid = spread(mid_row)
    tot = spread(tot_row)
    q_mid = qf * jnp.exp(cum - mid)
    k_mid = kf * jnp.exp(mid - cum)
    q_in = qf * jnp.exp(cum)
    k_out = kf * jnp.exp(tot - cum)

    heads = [slice(n * HEAD_DIM, (n + 1) * HEAD_DIM) for n in range(width // HEAD_DIM)]
    att = [jnp.where(causal, _dot_t(q_mid[:, hd], k_mid[:, hd]), 0.0) for hd in heads]
    intra = [_dot(a, v[:, hd]) for a, hd in zip(att, heads)]
    state = [state_ref[n] for n in range(len(heads))]
    inter = [[None] * nblk for _ in heads]
    for j in (range(nblk - 1, -1, -1) if reverse else range(nblk)):
        rows = slice(j * C_SUB, (j + 1) * C_SUB)
        for n, hd in enumerate(heads):
            inter[n][j] = _dot_t(q_in[rows, hd], state[n])
            state[n] = (state[n] * jnp.exp(cum[tot_row[j]:tot_row[j] + 1, hd])
                        + _tdot(v[rows, hd], k_out[rows, hd]))
    for n in range(len(heads)):
        state_ref[n] = state[n]
    o_ref[0] = jnp.concatenate([a + jnp.concatenate(b, axis=0) for a, b in zip(intra, inter)], axis=1)


def _hgrn(z, lb, *, th, reverse):
    bt, s, _ = z.shape
    nt = s // th
    groups = C_HEADS // C_HEADS_PER_STEP
    width = C_HEADS_PER_STEP * HEAD_DIM
    fcol = 2 * groups if reverse else groups
    tile = (lambda i: nt - 1 - i) if reverse else (lambda i: i)
    col = lambda c0: pl.BlockSpec((1, th, width), lambda b, h, i: (b, tile(i), c0 + h))
    kernel = functools.partial(_hgrn_kernel, th=th, reverse=reverse)
    return pl.pallas_call(
        kernel,
        grid=(bt, groups, nt),
        in_specs=[col(0), col(fcol), col(3 * groups), pl.BlockSpec((1, width), lambda b, h, i: (0, h))],
        out_specs=col(0),
        out_shape=jax.ShapeDtypeStruct((bt, s, C_WIDTH), F32),
        scratch_shapes=[pltpu.VMEM((C_HEADS_PER_STEP, HEAD_DIM, HEAD_DIM), F32)],
        compiler_params=_params("parallel", "parallel", "arbitrary"),
    )(z, z, z, lb.reshape(1, C_WIDTH))


D_HALO = SUBLANES
LRU_COLS = 512


def _lru_scan(a_ref, bx_ref, carry_ref, *, tl, reverse):
    run = tl // SUBLANES
    sub = lax.broadcasted_iota(jnp.int32, (SUBLANES, LRU_COLS), 0)
    for c0 in range(0, D_WIDTH, LRU_COLS):
        cols = pl.ds(c0, LRU_COLS)

        def first(n, x):
            hcur, prod = x
            rows = _step_rows(n, run, reverse)
            a = a_ref[rows, cols]
            return a * hcur + bx_ref[rows, cols], a * prod

        def second(n, hcur):
            rows = _step_rows(n, run, reverse)
            hcur = a_ref[rows, cols] * hcur + bx_ref[rows, cols]
            bx_ref[rows, cols] = hcur
            return hcur

        zero = jnp.zeros((SUBLANES, LRU_COLS), F32)
        hcur, prod = lax.fori_loop(0, run, first, (zero, zero + 1.0), unroll=SCAN_UNROLL)
        c = carry_ref[0:1, cols]
        start = zero
        for s_ in (range(SUBLANES - 1, -1, -1) if reverse else range(SUBLANES)):
            start = jnp.where(sub == s_, c, start)
            c = prod[s_:s_ + 1] * c + hcur[s_:s_ + 1]
        carry_ref[0:1, cols] = c
        lax.fori_loop(0, run, second, start, unroll=SCAN_UNROLL)


def _lru_kernel(xf_ref, xfp_ref, xfn_ref, xb_ref, xbp_ref, xbn_ref, cw_ref, cb_ref, w_ref, bias_ref, sp_ref,
                hf_ref, hb_ref, carry_ref, a_ref, bx_ref, *, tl, nt):
    i = pl.program_id(1)
    to_runs = _run_major(tl, True)
    to_time = _run_major(tl, False)

    @pl.when(i == 0)
    def _():
        carry_ref[...] = jnp.zeros(carry_ref.shape, F32)

    dirs = ((xf_ref, xfp_ref, xfn_ref, hf_ref, i), (xb_ref, xbp_ref, xbn_ref, hb_ref, nt - 1 - i))
    for di, (x_ref, xp_ref, xn_ref, h_ref, tile) in enumerate(dirs):
        x = x_ref[0]
        prev = jnp.where(tile > 0, xp_ref[0], 0.0)
        nxt = jnp.where(tile < nt - 1, xn_ref[0], 0.0)
        ext = jnp.concatenate([prev, x, nxt], axis=0)
        xc = cb_ref[...]
        for tap in range(4):
            xc = xc + cw_ref[tap:tap + 1, :] * ext[D_HALO - 1 + tap:D_HALO - 1 + tap + tl]
        xc = _permute_rows(to_runs, xc)
        gates = []
        for n in range(D_BLOCKS):
            cols = slice(n * D_BLOCK_DIM, (n + 1) * D_BLOCK_DIM)
            gates.append(_dot(xc[:, cols], w_ref[di, n]))
        r = jax.nn.sigmoid(jnp.concatenate([g[:, :D_BLOCK_DIM] for g in gates], axis=-1) + bias_ref[di, 0:1])
        ig = jax.nn.sigmoid(jnp.concatenate([g[:, D_BLOCK_DIM:] for g in gates], axis=-1) + bias_ref[di, 1:2])
        log_a = -LRU_C * r * sp_ref[di:di + 1]
        a = jnp.exp(log_a)
        a_ref[...] = a
        bx_ref[...] = jnp.sqrt(-jnp.tanh(log_a) * (a * a + 1.0)) * (ig * xc)
        _lru_scan(a_ref, bx_ref, carry_ref.at[di], tl=tl, reverse=(di == 1))
        h_ref[0] = _permute_rows(to_time, bx_ref[...])


def _rglru(z, conv_w, conv_b, wr, br, wi, bi, lam, *, tl):
    bt, s, _ = z.shape
    nt = s // tl
    xcol = 5 * C_WIDTH // D_WIDTH
    hpt = tl // D_HALO
    nh = s // D_HALO
    w = jnp.concatenate([wr, wi], axis=-1).astype(BF16)
    bias = jnp.stack([br, bi], axis=1).astype(F32)
    sp = jax.nn.softplus(-lam.astype(F32))
    fwd = lambda i: i
    bwd = lambda i: nt - 1 - i
    specs = []
    for tile in (fwd, bwd):
        specs += [
            pl.BlockSpec((1, tl, D_WIDTH), lambda b, i, tile=tile: (b, tile(i), xcol)),
            pl.BlockSpec((1, D_HALO, D_WIDTH),
                         lambda b, i, tile=tile: (b, jnp.maximum(tile(i) * hpt - 1, 0), xcol)),
            pl.BlockSpec((1, D_HALO, D_WIDTH),
                         lambda b, i, tile=tile: (b, jnp.minimum((tile(i) + 1) * hpt, nh - 1), xcol)),
        ]
    full = lambda shape: pl.BlockSpec(shape, lambda b, i: (0,) * len(shape))
    kernel = functools.partial(_lru_kernel, tl=tl, nt=nt)
    return pl.pallas_call(
        kernel,
        grid=(bt, nt),
        in_specs=specs + [full((4, D_WIDTH)), full((1, D_WIDTH)), full(w.shape), full(bias.shape), full(sp.shape)],
        out_specs=[pl.BlockSpec((1, tl, D_WIDTH), lambda b, i: (b, i, 0)),
                   pl.BlockSpec((1, tl, D_WIDTH), lambda b, i: (b, nt - 1 - i, 0))],
        out_shape=[jax.ShapeDtypeStruct((bt, s, D_WIDTH), F32)] * 2,
        scratch_shapes=[pltpu.VMEM((2, 1, D_WIDTH), F32), pltpu.VMEM((tl, D_WIDTH), F32),
                        pltpu.VMEM((tl, D_WIDTH), F32)],
        compiler_params=_params("parallel", "arbitrary"),
    )(z, z, z, z, z, z, conv_w.astype(F32), conv_b.reshape(1, -1).astype(F32), w, bias, sp)


TM_PROJ = 1024
TN_PROJ = 1024
TQ_ATTN = 1024
TS_S5 = 256
TM_OUT = 256
TM_XATTN = 512
TH_HGRN = 256
TL_LRU = 256


def _trunk(x, mem, p):
    bf = lambda t: t.astype(BF16)
    lb_all = jnp.cumsum(jax.nn.softmax(p['hgrn_lb_logits'].astype(F32), axis=0), axis=0)
    lb_all = lb_all - lb_all[0:1]
    slopes = jnp.asarray(2.0 ** (-8.0 * np.arange(1, A_HEADS + 1) / A_HEADS), dtype=F32)
    depth = p['norm_mix_pre'].shape[0]
    h = x
    for l in range(depth):
        if l % 2 == 0:
            e = l // 2
            z = _norm_matmul(h, p['norm_mix_pre'][l], bf(p['ev_w_in'][e]), tm=TM_PROJ, tn=TN_PROJ)
            y_a = _dilated_attention(z, slopes, tq=TQ_ATTN)
            tables = _s5_tables(p['s5_lam_re'][e], p['s5_lam_im'][e], p['s5_log_step'][e], p['s5_b_re'][e],
                                p['s5_b_im'][e], p['s5_c_re'][e], p['s5_c_im'][e], TS_S5 // SUBLANES)
            y_f, y_b = _s5(z, tables, ts=TS_S5)
            h = _even_out(h, y_a, y_f, y_b, z, p['s5_d'][e].astype(F32), bf(p['s5_glu_w'][e]),
                          p['s5_glu_b'][e].astype(F32), bf(p['ev_w_out'][e]), p['norm_mix_post'][l], tm=TM_OUT)
        else:
            o = l // 2
            z = _norm_matmul(h, p['norm_mix_pre'][l], bf(p['od_w_in'][o]), tm=TM_PROJ, tn=TN_PROJ)
            o_f = _hgrn(z, lb_all[l], th=TH_HGRN, reverse=False)
            o_b = _hgrn(z, lb_all[l], th=TH_HGRN, reverse=True)
            h_f, h_b = _rglru(z, p['lru_conv_w'][o], p['lru_conv_b'][o], p['lru_wr'][o], p['lru_br'][o],
                              p['lru_wi'][o], p['lru_bi'][o], p['lru_lambda'][o], tl=TL_LRU)
            h = _odd_out(h, o_f, o_b, h_f, h_b, z, p['hgrn_onorm'][o].astype(F32), bf(p['od_w_out'][o]),
                         p['norm_mix_post'][l], tm=TM_OUT)
        kv = _norm_matmul(mem, p['x_mem_norm'][l], bf(p['x_wkv'][l]), tm=N_MEM, tn=2 * X_WIDTH)
        h = _xattn(h, kv, p['norm_x_pre'][l], bf(p['x_wq'][l]), bf(p['x_wo'][l]), p['norm_x_post'][l], tm=TM_XATTN)
    return h


def kernel(x_prompt, x_sample, mem_prompt, mem_sample, norm_mix_pre, norm_mix_post, norm_x_pre, norm_x_post, ev_w_in, ev_w_out, s5_lam_re, s5_lam_im, s5_log_step, s5_b_re, s5_b_im, s5_c_re, s5_c_im, s5_d, s5_glu_w, s5_glu_b, od_w_in, od_w_out, hgrn_lb_logits, hgrn_onorm, lru_conv_w, lru_conv_b, lru_wr, lru_br, lru_wi, lru_bi, lru_lambda, x_wq, x_wkv, x_wo, x_mem_norm):
    params = dict(norm_mix_pre=norm_mix_pre, norm_mix_post=norm_mix_post, norm_x_pre=norm_x_pre,
                  norm_x_post=norm_x_post, ev_w_in=ev_w_in, ev_w_out=ev_w_out, s5_lam_re=s5_lam_re,
                  s5_lam_im=s5_lam_im, s5_log_step=s5_log_step, s5_b_re=s5_b_re, s5_b_im=s5_b_im,
                  s5_c_re=s5_c_re, s5_c_im=s5_c_im, s5_d=s5_d, s5_glu_w=s5_glu_w, s5_glu_b=s5_glu_b,
                  od_w_in=od_w_in, od_w_out=od_w_out, hgrn_lb_logits=hgrn_lb_logits, hgrn_onorm=hgrn_onorm,
                  lru_conv_w=lru_conv_w, lru_conv_b=lru_conv_b, lru_wr=lru_wr, lru_br=lru_br,
                  lru_wi=lru_wi, lru_bi=lru_bi, lru_lambda=lru_lambda, x_wq=x_wq, x_wkv=x_wkv,
                  x_wo=x_wo, x_mem_norm=x_mem_norm)
    return (_trunk(x_prompt, mem_prompt, params), _trunk(x_sample, mem_sample, params))
```

```python
import functools
import math

import jax
import jax.numpy as jnp
import numpy as np
from jax import lax
from jax.experimental import pallas as pl
from jax.experimental.pallas import tpu as pltpu

F32 = jnp.float32
BF16 = jnp.bfloat16

D_MODEL = 2048
N_MEM = 256
EPS = 1e-6
NEG_INF = -1e30

A_HEADS = 12
HEAD_DIM = 128
A_WIDTH = A_HEADS * HEAD_DIM
A_DILATIONS = (1, 4, 16)
A_RADIUS = 64
B_WIDTH = 512
B_GROUP = 16
B_GROUPS = B_WIDTH // B_GROUP
B_STATE = 64
B_STATES = B_GROUPS * B_STATE
C_HEADS = 8
C_WIDTH = C_HEADS * HEAD_DIM
C_SUB = 32
C_HEADS_PER_STEP = 8
D_WIDTH = 1024
D_BLOCKS = 8
D_BLOCK_DIM = D_WIDTH // D_BLOCKS
LRU_C = 8.0
X_HEADS = 4
X_WIDTH = X_HEADS * HEAD_DIM

EVEN_IN = 4 * A_WIDTH + 2 * B_WIDTH
ODD_IN = 5 * C_WIDTH + 2 * D_WIDTH

SUBLANES = 8
LANES = 128
VMEM_LIMIT = 56 * 1024 * 1024


def _params(*sem):
    return pltpu.CompilerParams(dimension_semantics=sem, vmem_limit_bytes=VMEM_LIMIT)


def _rms(x, g):
    return x * lax.rsqrt(jnp.mean(x * x, axis=-1, keepdims=True) + EPS) * g


def _dot(a, b):
    return jnp.dot(a.astype(BF16), b.astype(BF16), preferred_element_type=F32)


def _dot_t(a, b):
    return lax.dot_general(a.astype(BF16), b.astype(BF16), (((1,), (1,)), ((), ())),
                           preferred_element_type=F32)


def _tdot(a, b):
    return lax.dot_general(a.astype(BF16), b.astype(BF16), (((0,), (0,)), ((), ())),
                           preferred_element_type=F32)


def _norm_matmul_kernel(x_ref, g_ref, w_ref, o_ref, xn_ref):
    @pl.when(pl.program_id(2) == 0)
    def _():
        xn_ref[...] = _rms(x_ref[0], g_ref[...]).astype(BF16)

    o_ref[0] = jnp.dot(xn_ref[...], w_ref[...], preferred_element_type=F32).astype(o_ref.dtype)


def _norm_matmul(x, g, w, *, tm, tn, out_dtype=F32):
    bt, s, d = x.shape
    n = w.shape[1]
    return pl.pallas_call(
        _norm_matmul_kernel,
        grid=(bt, s // tm, n // tn),
        in_specs=[
            pl.BlockSpec((1, tm, d), lambda b, i, j: (b, i, 0)),
            pl.BlockSpec((1, d), lambda b, i, j: (0, 0)),
            pl.BlockSpec((d, tn), lambda b, i, j: (0, j)),
        ],
        out_specs=pl.BlockSpec((1, tm, tn), lambda b, i, j: (b, i, j)),
        out_shape=jax.ShapeDtypeStruct((bt, s, n), out_dtype),
        scratch_shapes=[pltpu.VMEM((tm, d), BF16)],
        compiler_params=_params("parallel", "parallel", "arbitrary"),
    )(x, g.reshape(1, d), w)


A_QB = 64
A_KW = 3 * A_QB
A_UNROLL = 8
A_SPLIT_DIL = A_DILATIONS[-1]
A_SHIFTS = (0, -A_RADIUS, -2 * A_RADIUS)


def _attn_kernel(slope_ref, q_ref, k_ref, v_ref, g_ref, o_ref, bias_ref, kd_ref, vd_ref, *stat_refs, tq, seq):
    t0 = pl.program_id(2) * tq
    stats = [stat_refs[3 * n:3 * n + 3] for n in range(len(A_DILATIONS))]

    @pl.when(pl.program_id(2) == 0)
    def _():
        slope = slope_ref[0]
        drel = (lax.broadcasted_iota(jnp.int32, (A_QB, A_KW), 1)
                - lax.broadcasted_iota(jnp.int32, (A_QB, A_KW), 0))
        for n, dil in enumerate(A_DILATIONS):
            for j, shift in enumerate(A_SHIFTS):
                dist = jnp.abs(drel + shift)
                bias_ref[n * len(A_SHIFTS) + j] = jnp.where(
                    dist <= A_RADIUS, -(slope * (dil * dist).astype(F32)), NEG_INF)

        def split(c, carry):
            dst = pl.ds(pl.multiple_of(c * A_QB, A_QB), A_QB)
            for r in range(A_SPLIT_DIL):
                src = pl.ds(r + pl.multiple_of(c * (A_QB * A_SPLIT_DIL), A_QB), A_QB, stride=A_SPLIT_DIL)
                kd_ref[r, dst, :] = k_ref[src, :].astype(BF16)
                vd_ref[r, dst, :] = v_ref[src, :].astype(BF16)
            return carry

        lax.fori_loop(0, seq // (A_QB * A_SPLIT_DIL), split, 0)

    for n, (dil, (m_ref, l_ref, acc_ref)) in enumerate(zip(A_DILATIONS, stats)):
        sub_len = seq // dil
        blocks = tq // (dil * A_QB)

        def steps(units, n=n, dil=dil, sub_len=sub_len, m_ref=m_ref, l_ref=l_ref, acc_ref=acc_ref):
            staged = []
            for r, c in units:
                qj = t0 // dil + c * A_QB
                kj = jnp.clip(qj - A_RADIUS, 0, sub_len - A_KW)
                rows = pl.ds(r + pl.multiple_of(c * (A_QB * dil), A_QB), A_QB, stride=dil)
                if dil == A_SPLIT_DIL:
                    krows = pl.ds(pl.multiple_of(kj, A_QB), A_KW)
                    keys, values = kd_ref.at[r], vd_ref.at[r]
                else:
                    krows = pl.ds(r + pl.multiple_of(kj * dil, A_QB), A_KW, stride=dil)
                    keys, values = k_ref, v_ref
                sc = _dot_t(q_ref[rows, :] * (HEAD_DIM ** -0.5), keys[krows, :])
                staged.append((rows, values, krows, sc + bias_ref[n * len(A_SHIFTS) + (qj - kj) // A_RADIUS]))
            probs = []
            for rows, values, krows, sc in staged:
                m = jnp.max(sc, axis=-1, keepdims=True)
                p = jnp.exp(sc - m)
                probs.append((m, jnp.sum(p, axis=-1, keepdims=True), _dot(p, values[krows, :])))
            for (rows, _, _, _), (m, l, acc) in zip(staged, probs):
                m_ref[rows, :] = jnp.broadcast_to(m, (A_QB, HEAD_DIM))
                l_ref[rows, :] = jnp.broadcast_to(l, (A_QB, HEAD_DIM))
                acc_ref[rows, :] = acc

        if dil <= A_UNROLL:
            per = A_UNROLL // dil

            def group_step(gi, carry, dil=dil, per=per, steps=steps):
                steps([(r, gi * per + cc) for cc in range(per) for r in range(dil)])
                return carry

            lax.fori_loop(0, blocks // per, group_step, 0)
        else:
            def block_step(c, carry, dil=dil, steps=steps):
                for r0 in range(0, dil, A_UNROLL):
                    steps([(r0 + u, c) for u in range(A_UNROLL)])
                return carry

            lax.fori_loop(jnp.int32(0), jnp.int32(blocks), block_step, 0)

    m_max = functools.reduce(jnp.maximum, [m_ref[...] for m_ref, _, _ in stats])
    num = 0.0
    den = 0.0
    for m_ref, l_ref, acc_ref in stats:
        w = jnp.exp(m_ref[...] - m_max)
        num = num + w * acc_ref[...]
        den = den + w * l_ref[...]
    o_ref[...] = (num / den * _silu(g_ref[...])).astype(o_ref.dtype)


def _dilated_attention(z, slopes, *, tq):
    bt, s, _ = z.shape
    assert s % tq == 0 and tq % (A_DILATIONS[-1] * A_QB) == 0 and s // A_DILATIONS[-1] >= A_KW
    kernel = functools.partial(_attn_kernel, tq=tq, seq=s)
    return pl.pallas_call(
        kernel,
        grid=(bt, A_HEADS, s // tq),
        in_specs=[
            pl.BlockSpec((1, 1, 1), lambda b, h, i: (h, 0, 0)),
            pl.BlockSpec((None, tq, HEAD_DIM), lambda b, h, i: (b, i, h)),
            pl.BlockSpec((None, s, HEAD_DIM), lambda b, h, i: (b, 0, A_HEADS + h)),
            pl.BlockSpec((None, s, HEAD_DIM), lambda b, h, i: (b, 0, 2 * A_HEADS + h)),
            pl.BlockSpec((None, tq, HEAD_DIM), lambda b, h, i: (b, i, 3 * A_HEADS + h)),
        ],
        out_specs=pl.BlockSpec((None, tq, HEAD_DIM), lambda b, h, i: (b, i, h)),
        out_shape=jax.ShapeDtypeStruct((bt, s, A_WIDTH), F32),
        scratch_shapes=([pltpu.VMEM((len(A_DILATIONS) * len(A_SHIFTS), A_QB, A_KW), F32)]
                        + [pltpu.VMEM((A_SPLIT_DIL, s // A_SPLIT_DIL, HEAD_DIM), BF16)] * 2
                        + [pltpu.VMEM((tq, HEAD_DIM), F32)] * (3 * len(A_DILATIONS))),
        compiler_params=_params("parallel", "parallel", "arbitrary"),
    )(slopes.reshape(A_HEADS, 1, 1), z, z, z, z)


S5_COLS = 512
S5_KB = 256
SCAN_UNROLL = 4


def _run_major(n, from_time):
    a = lax.broadcasted_iota(jnp.int32, (n, n), 0)
    b = lax.broadcasted_iota(jnp.int32, (n, n), 1)
    pos, t = (a, b) if from_time else (b, a)
    return (t == (pos % SUBLANES) * (n // SUBLANES) + pos // SUBLANES).astype(BF16)


def _permute_rows(m, x):
    hi = x.astype(BF16)
    lo = (x - hi.astype(F32)).astype(BF16)
    return jnp.dot(m, hi, preferred_element_type=F32) + jnp.dot(m, lo, preferred_element_type=F32)


def _step_rows(n, run, reverse):
    return pl.ds(pl.multiple_of((run - 1 - n if reverse else n) * SUBLANES, SUBLANES), SUBLANES)


def _s5_scan(bu_ref, a_ref, carry_ref, *, ts, reverse):
    run = ts // SUBLANES
    sub = lax.broadcasted_iota(jnp.int32, (SUBLANES, S5_COLS), 0)
    for c0 in range(0, B_STATES, S5_COLS):
        re = pl.ds(c0, S5_COLS)
        im = pl.ds(B_STATES + c0, S5_COLS)
        ar = jnp.broadcast_to(a_ref[0:1, re], (SUBLANES, S5_COLS))
        ai = jnp.broadcast_to(a_ref[0:1, im], (SUBLANES, S5_COLS))

        def sweep(xr, xi, store):
            def step(n, x):
                xr, xi = x
                rows = _step_rows(n, run, reverse)
                nr = ar * xr - ai * xi + bu_ref[rows, re]
                ni = ar * xi + ai * xr + bu_ref[rows, im]
                if store:
                    bu_ref[rows, re] = nr
                    bu_ref[rows, im] = ni
                return nr, ni
            return lax.fori_loop(0, run, step, (xr, xi), unroll=SCAN_UNROLL)

        zero = jnp.zeros((SUBLANES, S5_COLS), F32)
        er, ei = sweep(zero, zero, False)
        pr, pi = a_ref[1:2, re], a_ref[1:2, im]
        cr, ci = carry_ref[0:1, re], carry_ref[0:1, im]
        startr, starti = zero, zero
        for s_ in (range(SUBLANES - 1, -1, -1) if reverse else range(SUBLANES)):
            startr = jnp.where(sub == s_, cr, startr)
            starti = jnp.where(sub == s_, ci, starti)
            nr = pr * cr - pi * ci + er[s_:s_ + 1]
            ni = pr * ci + pi * cr + ei[s_:s_ + 1]
            cr, ci = nr, ni
        carry_ref[0:1, re] = cr
        carry_ref[0:1, im] = ci
        sweep(startr, starti, True)


def _s5_kernel(uf_ref, ub_ref, wb_ref, a_ref, wc_ref, yf_ref, yb_ref, carry_ref, bu_ref, *, ts):
    @pl.when(pl.program_id(1) == 0)
    def _():
        carry_ref[...] = jnp.zeros(carry_ref.shape, F32)

    to_runs = _run_major(ts, True)
    to_time = _run_major(ts, False)
    span = B_STATES * S5_KB // B_WIDTH
    blocks = [(slice(k0, k0 + S5_KB), [slice(part + k0 * span // S5_KB, part + k0 * span // S5_KB + span)
                                       for part in (0, B_STATES)])
              for k0 in range(0, B_WIDTH, S5_KB)]
    for di, (u_ref, y_ref) in enumerate(((uf_ref, yf_ref), (ub_ref, yb_ref))):
        u = jnp.dot(to_runs, u_ref[0].astype(BF16), preferred_element_type=F32).astype(BF16)
        for ucols, parts in blocks:
            for scols in parts:
                bu_ref[:, scols] = jnp.dot(u[:, ucols], wb_ref[di, ucols, scols], preferred_element_type=F32)
        _s5_scan(bu_ref, a_ref.at[di], carry_ref.at[di], ts=ts, reverse=(di == 1))
        y = [sum(_dot(bu_ref[:, scols], wc_ref[di, scols, ucols]) for scols in parts) for ucols, parts in blocks]
        y_ref[0] = _permute_rows(to_time, jnp.concatenate(y, axis=1))


def _s5_tables(lam_re, lam_im, log_step, b_re, b_im, c_re, c_im, run):
    lr, li = lam_re.astype(F32), lam_im.astype(F32)
    dt = jnp.exp(log_step.astype(F32))[..., None]
    e = jnp.exp(lr * dt)
    abar_re, abar_im = e * jnp.cos(li * dt), e * jnp.sin(li * dt)
    den = lr * lr + li * li
    nr, ni = abar_re - 1.0, abar_im
    fr, fi = (nr * lr + ni * li) / den, (ni * lr - nr * li) / den
    br, bi = b_re.astype(F32), b_im.astype(F32)
    bbar_re = fr[..., None] * br - fi[..., None] * bi
    bbar_im = fr[..., None] * bi + fi[..., None] * br
    eye = jnp.eye(B_GROUPS, dtype=F32)
    wb = jnp.concatenate([jnp.einsum('dgpc,gh->dgchp', t, eye).reshape(2, B_WIDTH, B_STATES)
                          for t in (bbar_re, bbar_im)], axis=-1)
    wc = jnp.concatenate([jnp.einsum('dgcp,gh->dhpgc', t, eye).reshape(2, B_STATES, B_WIDTH)
                          for t in (c_re.astype(F32), -c_im.astype(F32))], axis=1)
    pr, pi = abar_re, abar_im
    for _ in range(run - 1):
        pr, pi = pr * abar_re - pi * abar_im, pr * abar_im + pi * abar_re
    a = jnp.stack([jnp.concatenate([abar_re.reshape(2, B_STATES), abar_im.reshape(2, B_STATES)], axis=-1),
                   jnp.concatenate([pr.reshape(2, B_STATES), pi.reshape(2, B_STATES)], axis=-1)], axis=1)
    return wb.astype(BF16), a, wc.astype(BF16)


def _s5(z, tables, *, ts):
    bt, s, _ = z.shape
    wb, a, wc = tables
    nt = s // ts
    ucol = 4 * A_WIDTH // B_WIDTH
    kernel = functools.partial(_s5_kernel, ts=ts)
    full = lambda shape: pl.BlockSpec(shape, lambda b, i: (0,) * len(shape))
    return pl.pallas_call(
        kernel,
        grid=(bt, nt),
        in_specs=[
            pl.BlockSpec((1, ts, B_WIDTH), lambda b, i: (b, i, ucol)),
            pl.BlockSpec((1, ts, B_WIDTH), lambda b, i: (b, nt - 1 - i, ucol)),
            full(wb.shape), full(a.shape), full(wc.shape),
        ],
        out_specs=[
            pl.BlockSpec((1, ts, B_WIDTH), lambda b, i: (b, i, 0)),
            pl.BlockSpec((1, ts, B_WIDTH), lambda b, i: (b, nt - 1 - i, 0)),
        ],
        out_shape=[jax.ShapeDtypeStruct((bt, s, B_WIDTH), F32)] * 2,
        scratch_shapes=[pltpu.VMEM((2, 1, 2 * B_STATES), F32), pltpu.VMEM((ts, 2 * B_STATES), F32)],
        compiler_params=_params("parallel", "arbitrary"),
    )(z, z, wb, a, wc)


def _gelu_tanh(x):
    return 0.5 * x * (1.0 + jnp.tanh(math.sqrt(2.0 / math.pi) * (x + 0.044715 * (x * x * x))))


def _silu(x):
    return x * jax.nn.sigmoid(x)


def _even_out_kernel(h_ref, ya_ref, yf_ref, yb_ref, u_ref, gb_ref, d_ref, gw_ref, gbias_ref, wa_ref, wb_ref,
                     gpost_ref, *rest):
    *xattn_refs, o_ref = rest
    y_s5 = _gelu_tanh(d_ref[...] * u_ref[0] + yf_ref[0] + yb_ref[0])
    y_glu = y_s5 * jax.nn.sigmoid(jnp.dot(y_s5.astype(BF16), gw_ref[...], preferred_element_type=F32)
                                  + gbias_ref[...])
    y_b = y_glu * _silu(gb_ref[0])
    y = (jnp.dot(ya_ref[0].astype(BF16), wa_ref[...], preferred_element_type=F32)
         + jnp.dot(y_b.astype(BF16), wb_ref[...], preferred_element_type=F32))
    o_ref[0] = _xattn_block(h_ref[0] + _rms(y, gpost_ref[...]), *xattn_refs)


def _even_out(h, y_a, y_f, y_b, z, d_skip, glu_w, glu_b, w_out, g_post, xattn, *, tm):
    bt, s, d = h.shape
    ucol = 4 * A_WIDTH // B_WIDTH
    row = lambda width, col=0: pl.BlockSpec((1, tm, width), lambda b, i: (b, i, col))
    full = lambda shape: pl.BlockSpec(shape, lambda b, i: (0,) * len(shape))
    x_specs, x_args = _xattn_operands(*xattn)
    return pl.pallas_call(
        _even_out_kernel,
        grid=(bt, s // tm),
        in_specs=[row(d), row(A_WIDTH), row(B_WIDTH), row(B_WIDTH), row(B_WIDTH, ucol), row(B_WIDTH, ucol + 1),
                  full((1, B_WIDTH)), full((B_WIDTH, B_WIDTH)), full((1, B_WIDTH)),
                  full((A_WIDTH, d)), full((B_WIDTH, d)), full((1, d))] + x_specs,
        out_specs=row(d),
        out_shape=jax.ShapeDtypeStruct((bt, s, d), F32),
        compiler_params=_params("parallel", "parallel"),
    )(h, y_a, y_f, y_b, z, z, d_skip.reshape(1, -1), glu_w, glu_b.reshape(1, -1),
      w_out[:A_WIDTH], w_out[A_WIDTH:], g_post.reshape(1, -1), *x_args)


def _odd_out_kernel(h_ref, of_ref, ob_ref, gc_ref, hf_ref, hb_ref, gd_ref, onorm_ref, wc_ref, wd_ref, gpost_ref,
                    *rest):
    *xattn_refs, o_ref = rest
    o_c = of_ref[0] + ob_ref[0]
    heads = [o_c[:, hd * HEAD_DIM:(hd + 1) * HEAD_DIM] for hd in range(C_HEADS)]
    o_c = jnp.concatenate([t * lax.rsqrt(jnp.mean(t * t, axis=-1, keepdims=True) + EPS) for t in heads], axis=-1)
    y_c = o_c * onorm_ref[...] * _silu(gc_ref[0])
    y_d = (hf_ref[0] + hb_ref[0]) * _silu(gd_ref[0])
    y = (jnp.dot(y_c.astype(BF16), wc_ref[...], preferred_element_type=F32)
         + jnp.dot(y_d.astype(BF16), wd_ref[...], preferred_element_type=F32))
    o_ref[0] = _xattn_block(h_ref[0] + _rms(y, gpost_ref[...]), *xattn_refs)


def _odd_out(h, o_f, o_b, h_f, h_b, z, onorm, w_out, g_post, xattn, *, tm):
    bt, s, d = h.shape
    gccol = 4 * C_WIDTH // C_WIDTH
    gdcol = (5 * C_WIDTH + D_WIDTH) // D_WIDTH
    row = lambda width, col=0: pl.BlockSpec((1, tm, width), lambda b, i: (b, i, col))
    full = lambda shape: pl.BlockSpec(shape, lambda b, i: (0,) * len(shape))
    x_specs, x_args = _xattn_operands(*xattn)
    return pl.pallas_call(
        _odd_out_kernel,
        grid=(bt, s // tm),
        in_specs=[row(d), row(C_WIDTH), row(C_WIDTH), row(C_WIDTH, gccol),
                  row(D_WIDTH), row(D_WIDTH), row(D_WIDTH, gdcol),
                  full((1, C_WIDTH)), full((C_WIDTH, d)), full((D_WIDTH, d)), full((1, d))] + x_specs,
        out_specs=row(d),
        out_shape=jax.ShapeDtypeStruct((bt, s, d), F32),
        compiler_params=_params("parallel", "parallel"),
    )(h, o_f, o_b, z, h_f, h_b, z, onorm.reshape(1, -1), w_out[:C_WIDTH], w_out[C_WIDTH:],
      g_post.reshape(1, -1), *x_args)


def _xattn_block(h, gpre_ref, wq_ref, kv_ref, wo_ref, gpost_ref):
    q = _dot(_rms(h, gpre_ref[...]), wq_ref[...]) * (HEAD_DIM ** -0.5)
    heads = []
    for hd in range(X_HEADS):
        cols = slice(hd * HEAD_DIM, (hd + 1) * HEAD_DIM)
        k = kv_ref[0, :, cols]
        v = kv_ref[0, :, X_WIDTH + hd * HEAD_DIM:X_WIDTH + (hd + 1) * HEAD_DIM]
        sc = _dot_t(q[:, cols], k)
        p = jnp.exp(sc - jnp.max(sc, axis=-1, keepdims=True))
        heads.append(_dot(p, v) / jnp.sum(p, axis=-1, keepdims=True))
    o = jnp.concatenate(heads, axis=-1)
    return h + _rms(_dot(o, wo_ref[...]), gpost_ref[...])


def _xattn_operands(kv, g_pre, wq, wo, g_post):
    d = wq.shape[0]
    full = lambda shape: pl.BlockSpec(shape, lambda b, i: (0,) * len(shape))
    specs = [full((1, d)), full((d, X_WIDTH)), pl.BlockSpec((1, N_MEM, 2 * X_WIDTH), lambda b, i: (b, 0, 0)),
             full((X_WIDTH, d)), full((1, d))]
    return specs, (g_pre.reshape(1, -1), wq, kv, wo, g_post.reshape(1, -1))


def _hgrn_kernel(q_ref, f_ref, v_ref, lb_ref, o_ref, state_ref, *, th, reverse):
    @pl.when(pl.program_id(2) == 0)
    def _():
        state_ref[...] = jnp.zeros(state_ref.shape, F32)

    lb = lb_ref[...]
    fx = f_ref[0]
    qf = _silu(q_ref[0])
    v = v_ref[0]
    logf = jnp.log(lb + (1.0 - lb) * jax.nn.sigmoid(fx))
    kf = (1.0 - lb) * jax.nn.sigmoid(-fx)

    t = lax.broadcasted_iota(jnp.int32, (th, th), 0)
    r = lax.broadcasted_iota(jnp.int32, (th, th), 1)
    same = (t // C_SUB) == (r // C_SUB)
    causal = same & ((r >= t) if reverse else (r <= t))
    cum = _permute_rows(causal.astype(BF16), logf)
    nblk = th // C_SUB
    half = C_SUB // 2
    mid_row = [j * C_SUB + (half if reverse else half - 1) for j in range(nblk)]
    tot_row = [j * C_SUB + (0 if reverse else C_SUB - 1) for j in range(nblk)]
    width = cum.shape[1]
    spread = lambda rows: jnp.concatenate(
        [jnp.broadcast_to(cum[i:i + 1], (C_SUB, width)) for i in rows], axis=0)
    mid = spread(mid_row)
    tot = spread(tot_row)
    q_mid = qf * jnp.exp(cum - mid)
    k_mid = kf * jnp.exp(mid - cum)
    q_in = qf * jnp.exp(cum)
    k_out = kf * jnp.exp(tot - cum)

    heads = [slice(n * HEAD_DIM, (n + 1) * HEAD_DIM) for n in range(width // HEAD_DIM)]
    att = [jnp.where(causal, _dot_t(q_mid[:, hd], k_mid[:, hd]), 0.0) for hd in heads]
    intra = [_dot(a, v[:, hd]) for a, hd in zip(att, heads)]
    state = [state_ref[n] for n in range(len(heads))]
    inter = [[None] * nblk for _ in heads]
    for j in (range(nblk - 1, -1, -1) if reverse else range(nblk)):
        rows = slice(j * C_SUB, (j + 1) * C_SUB)
        for n, hd in enumerate(heads):
            inter[n][j] = _dot_t(q_in[rows, hd], state[n])
            state[n] = (state[n] * jnp.exp(cum[tot_row[j]:tot_row[j] + 1, hd])
                        + _tdot(v[rows, hd], k_out[rows, hd]))
    for n in range(len(heads)):
        state_ref[n] = state[n]
    o_ref[0] = jnp.concatenate([a + jnp.concatenate(b, axis=0) for a, b in zip(intra, inter)], axis=1)


def _hgrn(z, lb, *, th, reverse):
    bt, s, _ = z.shape
    nt = s // th
    groups = C_HEADS // C_HEADS_PER_STEP
    width = C_HEADS_PER_STEP * HEAD_DIM
    fcol = 2 * groups if reverse else groups
    tile = (lambda i: nt - 1 - i) if reverse else (lambda i: i)
    col = lambda c0: pl.BlockSpec((1, th, width), lambda b, h, i: (b, tile(i), c0 + h))
    kernel = functools.partial(_hgrn_kernel, th=th, reverse=reverse)
    return pl.pallas_call(
        kernel,
        grid=(bt, groups, nt),
        in_specs=[col(0), col(fcol), col(3 * groups), pl.BlockSpec((1, width), lambda b, h, i: (0, h))],
        out_specs=col(0),
        out_shape=jax.ShapeDtypeStruct((bt, s, C_WIDTH), F32),
        scratch_shapes=[pltpu.VMEM((C_HEADS_PER_STEP, HEAD_DIM, HEAD_DIM), F32)],
        compiler_params=_params("parallel", "parallel", "arbitrary"),
    )(z, z, z, lb.reshape(1, C_WIDTH))


D_HALO = SUBLANES
LRU_COLS = 512


def _lru_scan(a_ref, bx_ref, carry_ref, *, tl, reverse):
    run = tl // SUBLANES
    sub = lax.broadcasted_iota(jnp.int32, (SUBLANES, LRU_COLS), 0)
    for c0 in range(0, D_WIDTH, LRU_COLS):
        cols = pl.ds(c0, LRU_COLS)

        def first(n, x):
            hcur, prod = x
            rows = _step_rows(n, run, reverse)
            a = a_ref[rows, cols]
            return a * hcur + bx_ref[rows, cols], a * prod

        def second(n, hcur):
            rows = _step_rows(n, run, reverse)
            hcur = a_ref[rows, cols] * hcur + bx_ref[rows, cols]
            bx_ref[rows, cols] = hcur
            return hcur

        zero = jnp.zeros((SUBLANES, LRU_COLS), F32)
        hcur, prod = lax.fori_loop(0, run, first, (zero, zero + 1.0), unroll=SCAN_UNROLL)
        c = carry_ref[0:1, cols]
        start = zero
        for s_ in (range(SUBLANES - 1, -1, -1) if reverse else range(SUBLANES)):
            start = jnp.where(sub == s_, c, start)
            c = prod[s_:s_ + 1] * c + hcur[s_:s_ + 1]
        carry_ref[0:1, cols] = c
        lax.fori_loop(0, run, second, start, unroll=SCAN_UNROLL)


def _lru_kernel(xf_ref, xfp_ref, xfn_ref, xb_ref, xbp_ref, xbn_ref, cw_ref, cb_ref, w_ref, bias_ref, sp_ref,
                hf_ref, hb_ref, carry_ref, a_ref, bx_ref, *, tl, nt):
    i = pl.program_id(1)
    to_runs = _run_major(tl, True)
    to_time = _run_major(tl, False)

    @pl.when(i == 0)
    def _():
        carry_ref[...] = jnp.zeros(carry_ref.shape, F32)

    dirs = ((xf_ref, xfp_ref, xfn_ref, hf_ref, i), (xb_ref, xbp_ref, xbn_ref, hb_ref, nt - 1 - i))
    for di, (x_ref, xp_ref, xn_ref, h_ref, tile) in enumerate(dirs):
        x = x_ref[0]
        prev = jnp.where(tile > 0, xp_ref[0], 0.0)
        nxt = jnp.where(tile < nt - 1, xn_ref[0], 0.0)
        ext = jnp.concatenate([prev, x, nxt], axis=0)
        xc = cb_ref[...]
        for tap in range(4):
            xc = xc + cw_ref[tap:tap + 1, :] * ext[D_HALO - 1 + tap:D_HALO - 1 + tap + tl]
        xc = _permute_rows(to_runs, xc)
        gates = []
        for n in range(D_BLOCKS):
            cols = slice(n * D_BLOCK_DIM, (n + 1) * D_BLOCK_DIM)
            gates.append(_dot(xc[:, cols], w_ref[di, n]))
        r = jax.nn.sigmoid(jnp.concatenate([g[:, :D_BLOCK_DIM] for g in gates], axis=-1) + bias_ref[di, 0:1])
        ig = jax.nn.sigmoid(jnp.concatenate([g[:, D_BLOCK_DIM:] for g in gates], axis=-1) + bias_ref[di, 1:2])
        log_a = -LRU_C * r * sp_ref[di:di + 1]
        a = jnp.exp(log_a)
        a_ref[...] = a
        bx_ref[...] = jnp.sqrt(-jnp.tanh(log_a) * (a * a + 1.0)) * (ig * xc)
        _lru_scan(a_ref, bx_ref, carry_ref.at[di], tl=tl, reverse=(di == 1))
        h_ref[0] = _permute_rows(to_time, bx_ref[...])


def _rglru(z, conv_w, conv_b, wr, br, wi, bi, lam, *, tl):
    bt, s, _ = z.shape
    nt = s // tl
    xcol = 5 * C_WIDTH // D_WIDTH
    hpt = tl // D_HALO
    nh = s // D_HALO
    w = jnp.concatenate([wr, wi], axis=-1).astype(BF16)
    bias = jnp.stack([br, bi], axis=1).astype(F32)
    sp = jax.nn.softplus(-lam.astype(F32))
    fwd = lambda i: i
    bwd = lambda i: nt - 1 - i
    specs = []
    for tile in (fwd, bwd):
        specs += [
            pl.BlockSpec((1, tl, D_WIDTH), lambda b, i, tile=tile: (b, tile(i), xcol)),
            pl.BlockSpec((1, D_HALO, D_WIDTH),
                         lambda b, i, tile=tile: (b, jnp.maximum(tile(i) * hpt - 1, 0), xcol)),
            pl.BlockSpec((1, D_HALO, D_WIDTH),
                         lambda b, i, tile=tile: (b, jnp.minimum((tile(i) + 1) * hpt, nh - 1), xcol)),
        ]
    full = lambda shape: pl.BlockSpec(shape, lambda b, i: (0,) * len(shape))
    kernel = functools.partial(_lru_kernel, tl=tl, nt=nt)
    return pl.pallas_call(
        kernel,
        grid=(bt, nt),
        in_specs=specs + [full((4, D_WIDTH)), full((1, D_WIDTH)), full(w.shape), full(bias.shape), full(sp.shape)],
        out_specs=[pl.BlockSpec((1, tl, D_WIDTH), lambda b, i: (b, i, 0)),
                   pl.BlockSpec((1, tl, D_WIDTH), lambda b, i: (b, nt - 1 - i, 0))],
        out_shape=[jax.ShapeDtypeStruct((bt, s, D_WIDTH), F32)] * 2,
        scratch_shapes=[pltpu.VMEM((2, 1, D_WIDTH), F32), pltpu.VMEM((tl, D_WIDTH), F32),
                        pltpu.VMEM((tl, D_WIDTH), F32)],
        compiler_params=_params("parallel", "arbitrary"),
    )(z, z, z, z, z, z, conv_w.astype(F32), conv_b.reshape(1, -1).astype(F32), w, bias, sp)


TM_PROJ = 1024
TN_PROJ = 1024
TQ_ATTN = 1024
TS_S5 = 256
TM_OUT = 256
TH_HGRN = 256
TL_LRU = 256


def _trunk(x, mem, p):
    bf = lambda t: t.astype(BF16)
    lb_all = jnp.cumsum(jax.nn.softmax(p['hgrn_lb_logits'].astype(F32), axis=0), axis=0)
    lb_all = lb_all - lb_all[0:1]
    slopes = jnp.asarray(2.0 ** (-8.0 * np.arange(1, A_HEADS + 1) / A_HEADS), dtype=F32)
    depth = p['norm_mix_pre'].shape[0]
    h = x
    for l in range(depth):
        kv = _norm_matmul(mem, p['x_mem_norm'][l], bf(p['x_wkv'][l]), tm=N_MEM, tn=2 * X_WIDTH)
        xattn = (kv, p['norm_x_pre'][l], bf(p['x_wq'][l]), bf(p['x_wo'][l]), p['norm_x_post'][l])
        if l % 2 == 0:
            e = l // 2
            z = _norm_matmul(h, p['norm_mix_pre'][l], bf(p['ev_w_in'][e]), tm=TM_PROJ, tn=TN_PROJ)
            y_a = _dilated_attention(z, slopes, tq=TQ_ATTN)
            tables = _s5_tables(p['s5_lam_re'][e], p['s5_lam_im'][e], p['s5_log_step'][e], p['s5_b_re'][e],
                                p['s5_b_im'][e], p['s5_c_re'][e], p['s5_c_im'][e], TS_S5 // SUBLANES)
            y_f, y_b = _s5(z, tables, ts=TS_S5)
            h = _even_out(h, y_a, y_f, y_b, z, p['s5_d'][e].astype(F32), bf(p['s5_glu_w'][e]),
                          p['s5_glu_b'][e].astype(F32), bf(p['ev_w_out'][e]), p['norm_mix_post'][l], xattn,
                          tm=TM_OUT)
        else:
            o = l // 2
            z = _norm_matmul(h, p['norm_mix_pre'][l], bf(p['od_w_in'][o]), tm=TM_PROJ, tn=TN_PROJ)
            o_f = _hgrn(z, lb_all[l], th=TH_HGRN, reverse=False)
            o_b = _hgrn(z, lb_all[l], th=TH_HGRN, reverse=True)
            h_f, h_b = _rglru(z, p['lru_conv_w'][o], p['lru_conv_b'][o], p['lru_wr'][o], p['lru_br'][o],
                              p['lru_wi'][o], p['lru_bi'][o], p['lru_lambda'][o], tl=TL_LRU)
            h = _odd_out(h, o_f, o_b, h_f, h_b, z, p['hgrn_onorm'][o].astype(F32), bf(p['od_w_out'][o]),
                         p['norm_mix_post'][l], xattn, tm=TM_OUT)
    return h


def kernel(x_prompt, x_sample, mem_prompt, mem_sample, norm_mix_pre, norm_mix_post, norm_x_pre, norm_x_post, ev_w_in, ev_w_out, s5_lam_re, s5_lam_im, s5_log_step, s5_b_re, s5_b_im, s5_c_re, s5_c_im, s5_d, s5_glu_w, s5_glu_b, od_w_in, od_w_out, hgrn_lb_logits, hgrn_onorm, lru_conv_w, lru_conv_b, lru_wr, lru_br, lru_wi, lru_bi, lru_lambda, x_wq, x_wkv, x_wo, x_mem_norm):
    params = dict(norm_mix_pre=norm_mix_pre, norm_mix_post=norm_mix_post, norm_x_pre=norm_x_pre,
                  norm_x_post=norm_x_post, ev_w_in=ev_w_in, ev_w_out=ev_w_out, s5_lam_re=s5_lam_re,
                  s5_lam_im=s5_lam_im, s5_log_step=s5_log_step, s5_b_re=s5_b_re, s5_b_im=s5_b_im,
                  s5_c_re=s5_c_re, s5_c_im=s5_c_im, s5_d=s5_d, s5_glu_w=s5_glu_w, s5_glu_b=s5_glu_b,
                  od_w_in=od_w_in, od_w_out=od_w_out, hgrn_lb_logits=hgrn_lb_logits, hgrn_onorm=hgrn_onorm,
                  lru_conv_w=lru_conv_w, lru_conv_b=lru_conv_b, lru_wr=lru_wr, lru_br=lru_br,
                  lru_wi=lru_wi, lru_bi=lru_bi, lru_lambda=lru_lambda, x_wq=x_wq, x_wkv=x_wkv,
                  x_wo=x_wo, x_mem_norm=x_mem_norm)
    return (_trunk(x_prompt, mem_prompt, params), _trunk(x_sample, mem_sample, params))
```

```python
import functools
import math

import jax
import jax.numpy as jnp
import numpy as np
from jax import lax
from jax.experimental import pallas as pl
from jax.experimental.pallas import tpu as pltpu

F32 = jnp.float32
BF16 = jnp.bfloat16

D_MODEL = 2048
N_MEM = 256
EPS = 1e-6
NEG_INF = -1e30

A_HEADS = 12
HEAD_DIM = 128
A_WIDTH = A_HEADS * HEAD_DIM
A_DILATIONS = (1, 4, 16)
A_RADIUS = 64
B_WIDTH = 512
B_GROUP = 16
B_GROUPS = B_WIDTH // B_GROUP
B_STATE = 64
B_STATES = B_GROUPS * B_STATE
C_HEADS = 8
C_WIDTH = C_HEADS * HEAD_DIM
C_SUB = 32
C_HEADS_PER_STEP = 8
D_WIDTH = 1024
D_BLOCKS = 8
D_BLOCK_DIM = D_WIDTH // D_BLOCKS
LRU_C = 8.0
X_HEADS = 4
X_WIDTH = X_HEADS * HEAD_DIM

EVEN_IN = 4 * A_WIDTH + 2 * B_WIDTH
ODD_IN = 5 * C_WIDTH + 2 * D_WIDTH

SUBLANES = 8
LANES = 128
VMEM_LIMIT = 56 * 1024 * 1024


def _params(*sem):
    return pltpu.CompilerParams(dimension_semantics=sem, vmem_limit_bytes=VMEM_LIMIT)


def _rms(x, g):
    return x * lax.rsqrt(jnp.mean(x * x, axis=-1, keepdims=True) + EPS) * g


def _dot(a, b):
    return jnp.dot(a.astype(BF16), b.astype(BF16), preferred_element_type=F32)


def _dot_t(a, b):
    return lax.dot_general(a.astype(BF16), b.astype(BF16), (((1,), (1,)), ((), ())),
                           preferred_element_type=F32)


def _tdot(a, b):
    return lax.dot_general(a.astype(BF16), b.astype(BF16), (((0,), (0,)), ((), ())),
                           preferred_element_type=F32)


def _norm_matmul_kernel(x_ref, g_ref, w_ref, o_ref, xn_ref):
    @pl.when(pl.program_id(2) == 0)
    def _():
        xn_ref[...] = _rms(x_ref[0], g_ref[...]).astype(BF16)

    o_ref[0] = jnp.dot(xn_ref[...], w_ref[...], preferred_element_type=F32).astype(o_ref.dtype)


def _norm_matmul(x, g, w, *, tm, tn, out_dtype=F32):
    bt, s, d = x.shape
    n = w.shape[1]
    return pl.pallas_call(
        _norm_matmul_kernel,
        grid=(bt, s // tm, n // tn),
        in_specs=[
            pl.BlockSpec((1, tm, d), lambda b, i, j: (b, i, 0)),
            pl.BlockSpec((1, d), lambda b, i, j: (0, 0)),
            pl.BlockSpec((d, tn), lambda b, i, j: (0, j)),
        ],
        out_specs=pl.BlockSpec((1, tm, tn), lambda b, i, j: (b, i, j)),
        out_shape=jax.ShapeDtypeStruct((bt, s, n), out_dtype),
        scratch_shapes=[pltpu.VMEM((tm, d), BF16)],
        compiler_params=_params("parallel", "parallel", "arbitrary"),
    )(x, g.reshape(1, d), w)


A_QB = 64
A_KW = 3 * A_QB
A_UNROLL = 16
A_SPLIT_DIL = A_DILATIONS[-1]
A_SHIFTS = (0, -A_RADIUS, -2 * A_RADIUS)


def _attn_kernel(slope_ref, q_ref, k_ref, v_ref, g_ref, o_ref, bias_ref, kd_ref, vd_ref, *stat_refs, tq, seq):
    t0 = pl.program_id(2) * tq
    stats = [stat_refs[3 * n:3 * n + 3] for n in range(len(A_DILATIONS))]
    ones = jnp.ones((A_KW, HEAD_DIM), BF16)

    @pl.when(pl.program_id(2) == 0)
    def _():
        slope = slope_ref[0]
        drel = (lax.broadcasted_iota(jnp.int32, (A_QB, A_KW), 1)
                - lax.broadcasted_iota(jnp.int32, (A_QB, A_KW), 0))
        for n, dil in enumerate(A_DILATIONS):
            for j, shift in enumerate(A_SHIFTS):
                dist = jnp.abs(drel + shift)
                bias_ref[n * len(A_SHIFTS) + j] = jnp.where(
                    dist <= A_RADIUS, -(slope * (dil * dist).astype(F32)), NEG_INF)

        def split(c, carry):
            dst = pl.ds(pl.multiple_of(c * A_QB, A_QB), A_QB)
            for r in range(A_SPLIT_DIL):
                src = pl.ds(r + pl.multiple_of(c * (A_QB * A_SPLIT_DIL), A_QB), A_QB, stride=A_SPLIT_DIL)
                kd_ref[r, dst, :] = k_ref[src, :].astype(BF16)
                vd_ref[r, dst, :] = v_ref[src, :].astype(BF16)
            return carry

        lax.fori_loop(0, seq // (A_QB * A_SPLIT_DIL), split, 0)

    for n, (dil, (m_ref, l_ref, acc_ref)) in enumerate(zip(A_DILATIONS, stats)):
        sub_len = seq // dil
        blocks = tq // (dil * A_QB)

        def steps(units, n=n, dil=dil, sub_len=sub_len, m_ref=m_ref, l_ref=l_ref, acc_ref=acc_ref):
            staged = []
            for r, c in units:
                qj = t0 // dil + c * A_QB
                kj = jnp.clip(qj - A_RADIUS, 0, sub_len - A_KW)
                rows = pl.ds(r + pl.multiple_of(c * (A_QB * dil), A_QB), A_QB, stride=dil)
                if dil == A_SPLIT_DIL:
                    krows = pl.ds(pl.multiple_of(kj, A_QB), A_KW)
                    keys, values = kd_ref.at[r], vd_ref.at[r]
                else:
                    krows = pl.ds(r + pl.multiple_of(kj * dil, A_QB), A_KW, stride=dil)
                    keys, values = k_ref, v_ref
                sc = _dot_t(q_ref[rows, :] * (HEAD_DIM ** -0.5), keys[krows, :])
                staged.append((rows, values, krows, sc + bias_ref[n * len(A_SHIFTS) + (qj - kj) // A_RADIUS]))
            probs = []
            for rows, values, krows, sc in staged:
                m = jnp.max(sc, axis=-1, keepdims=True)
                p = jnp.exp(sc - m)
                probs.append((m, _dot(p, jnp.concatenate([values[krows, :].astype(BF16), ones], axis=1))))
            for (rows, _, _, _), (m, acc_l) in zip(staged, probs):
                m_ref[rows, :] = jnp.broadcast_to(m, (A_QB, HEAD_DIM))
                l_ref[rows, :] = acc_l[:, HEAD_DIM:]
                acc_ref[rows, :] = acc_l[:, :HEAD_DIM]

        if dil <= A_UNROLL:
            per = A_UNROLL // dil

            def group_step(gi, carry, dil=dil, per=per, steps=steps):
                steps([(r, gi * per + cc) for cc in range(per) for r in range(dil)])
                return carry

            lax.fori_loop(jnp.int32(0), jnp.int32(blocks // per), group_step, 0)
        else:
            def block_step(c, carry, dil=dil, steps=steps):
                for r0 in range(0, dil, A_UNROLL):
                    steps([(r0 + u, c) for u in range(A_UNROLL)])
                return carry

            lax.fori_loop(jnp.int32(0), jnp.int32(blocks), block_step, 0)

    m_max = functools.reduce(jnp.maximum, [m_ref[...] for m_ref, _, _ in stats])
    num = 0.0
    den = 0.0
    for m_ref, l_ref, acc_ref in stats:
        w = jnp.exp(m_ref[...] - m_max)
        num = num + w * acc_ref[...]
        den = den + w * l_ref[...]
    o_ref[...] = (num / den * _silu(g_ref[...])).astype(o_ref.dtype)


def _dilated_attention(z, slopes, *, tq):
    bt, s, _ = z.shape
    assert s % tq == 0 and tq % (A_DILATIONS[-1] * A_QB) == 0 and s // A_DILATIONS[-1] >= A_KW
    kernel = functools.partial(_attn_kernel, tq=tq, seq=s)
    return pl.pallas_call(
        kernel,
        grid=(bt, A_HEADS, s // tq),
        in_specs=[
            pl.BlockSpec((1, 1, 1), lambda b, h, i: (h, 0, 0)),
            pl.BlockSpec((None, tq, HEAD_DIM), lambda b, h, i: (b, i, h)),
            pl.BlockSpec((None, s, HEAD_DIM), lambda b, h, i: (b, 0, A_HEADS + h)),
            pl.BlockSpec((None, s, HEAD_DIM), lambda b, h, i: (b, 0, 2 * A_HEADS + h)),
            pl.BlockSpec((None, tq, HEAD_DIM), lambda b, h, i: (b, i, 3 * A_HEADS + h)),
        ],
        out_specs=pl.BlockSpec((None, tq, HEAD_DIM), lambda b, h, i: (b, i, h)),
        out_shape=jax.ShapeDtypeStruct((bt, s, A_WIDTH), F32),
        scratch_shapes=([pltpu.VMEM((len(A_DILATIONS) * len(A_SHIFTS), A_QB, A_KW), F32)]
                        + [pltpu.VMEM((A_SPLIT_DIL, s // A_SPLIT_DIL, HEAD_DIM), BF16)] * 2
                        + [pltpu.VMEM((tq, HEAD_DIM), F32)] * (3 * len(A_DILATIONS))),
        compiler_params=_params("parallel", "parallel", "arbitrary"),
    )(slopes.reshape(A_HEADS, 1, 1), z, z, z, z)


S5_COLS = 512
S5_KB = 256
SCAN_UNROLL = 4


def _run_major(n, from_time):
    a = lax.broadcasted_iota(jnp.int32, (n, n), 0)
    b = lax.broadcasted_iota(jnp.int32, (n, n), 1)
    pos, t = (a, b) if from_time else (b, a)
    return (t == (pos % SUBLANES) * (n // SUBLANES) + pos // SUBLANES).astype(BF16)


def _permute_rows(m, x):
    hi = x.astype(BF16)
    lo = (x - hi.astype(F32)).astype(BF16)
    return jnp.dot(m, hi, preferred_element_type=F32) + jnp.dot(m, lo, preferred_element_type=F32)


def _step_rows(n, run, reverse):
    return pl.ds(pl.multiple_of((run - 1 - n if reverse else n) * SUBLANES, SUBLANES), SUBLANES)


def _s5_scan(bu_ref, a_ref, carry_ref, *, ts, reverse):
    run = ts // SUBLANES
    sub = lax.broadcasted_iota(jnp.int32, (SUBLANES, S5_COLS), 0)
    for c0 in range(0, B_STATES, S5_COLS):
        re = pl.ds(c0, S5_COLS)
        im = pl.ds(B_STATES + c0, S5_COLS)
        ar = jnp.broadcast_to(a_ref[0:1, re], (SUBLANES, S5_COLS))
        ai = jnp.broadcast_to(a_ref[0:1, im], (SUBLANES, S5_COLS))

        def sweep(xr, xi, store):
            def step(n, x):
                xr, xi = x
                rows = _step_rows(n, run, reverse)
                nr = ar * xr - ai * xi + bu_ref[rows, re]
                ni = ar * xi + ai * xr + bu_ref[rows, im]
                if store:
                    bu_ref[rows, re] = nr
                    bu_ref[rows, im] = ni
                return nr, ni
            return lax.fori_loop(0, run, step, (xr, xi), unroll=SCAN_UNROLL)

        zero = jnp.zeros((SUBLANES, S5_COLS), F32)
        er, ei = sweep(zero, zero, False)
        pr, pi = a_ref[1:2, re], a_ref[1:2, im]
        cr, ci = carry_ref[0:1, re], carry_ref[0:1, im]
        startr, starti = zero, zero
        for s_ in (range(SUBLANES - 1, -1, -1) if reverse else range(SUBLANES)):
            startr = jnp.where(sub == s_, cr, startr)
            starti = jnp.where(sub == s_, ci, starti)
            nr = pr * cr - pi * ci + er[s_:s_ + 1]
            ni = pr * ci + pi * cr + ei[s_:s_ + 1]
            cr, ci = nr, ni
        carry_ref[0:1, re] = cr
        carry_ref[0:1, im] = ci
        sweep(startr, starti, True)


def _s5_kernel(uf_ref, ub_ref, wb_ref, a_ref, wc_ref, yf_ref, yb_ref, carry_ref, bu_ref, *, ts):
    @pl.when(pl.program_id(1) == 0)
    def _():
        carry_ref[...] = jnp.zeros(carry_ref.shape, F32)

    to_runs = _run_major(ts, True)
    to_time = _run_major(ts, False)
    span = B_STATES * S5_KB // B_WIDTH
    blocks = [(slice(k0, k0 + S5_KB), [slice(part + k0 * span // S5_KB, part + k0 * span // S5_KB + span)
                                       for part in (0, B_STATES)])
              for k0 in range(0, B_WIDTH, S5_KB)]
    for di, (u_ref, y_ref) in enumerate(((uf_ref, yf_ref), (ub_ref, yb_ref))):
        u = jnp.dot(to_runs, u_ref[0].astype(BF16), preferred_element_type=F32).astype(BF16)
        for ucols, parts in blocks:
            for scols in parts:
                bu_ref[:, scols] = jnp.dot(u[:, ucols], wb_ref[di, ucols, scols], preferred_element_type=F32)
        _s5_scan(bu_ref, a_ref.at[di], carry_ref.at[di], ts=ts, reverse=(di == 1))
        y = [sum(_dot(bu_ref[:, scols], wc_ref[di, scols, ucols]) for scols in parts) for ucols, parts in blocks]
        y_ref[0] = _permute_rows(to_time, jnp.concatenate(y, axis=1))


def _s5_tables(lam_re, lam_im, log_step, b_re, b_im, c_re, c_im, run):
    lr, li = lam_re.astype(F32), lam_im.astype(F32)
    dt = jnp.exp(log_step.astype(F32))[..., None]
    e = jnp.exp(lr * dt)
    abar_re, abar_im = e * jnp.cos(li * dt), e * jnp.sin(li * dt)
    den = lr * lr + li * li
    nr, ni = abar_re - 1.0, abar_im
    fr, fi = (nr * lr + ni * li) / den, (ni * lr - nr * li) / den
    br, bi = b_re.astype(F32), b_im.astype(F32)
    bbar_re = fr[..., None] * br - fi[..., None] * bi
    bbar_im = fr[..., None] * bi + fi[..., None] * br
    eye = jnp.eye(B_GROUPS, dtype=F32)
    wb = jnp.concatenate([jnp.einsum('dgpc,gh->dgchp', t, eye).reshape(2, B_WIDTH, B_STATES)
                          for t in (bbar_re, bbar_im)], axis=-1)
    wc = jnp.concatenate([jnp.einsum('dgcp,gh->dhpgc', t, eye).reshape(2, B_STATES, B_WIDTH)
                          for t in (c_re.astype(F32), -c_im.astype(F32))], axis=1)
    pr, pi = abar_re, abar_im
    for _ in range(run - 1):
        pr, pi = pr * abar_re - pi * abar_im, pr * abar_im + pi * abar_re
    a = jnp.stack([jnp.concatenate([abar_re.reshape(2, B_STATES), abar_im.reshape(2, B_STATES)], axis=-1),
                   jnp.concatenate([pr.reshape(2, B_STATES), pi.reshape(2, B_STATES)], axis=-1)], axis=1)
    return wb.astype(BF16), a, wc.astype(BF16)


def _s5(z, tables, *, ts):
    bt, s, _ = z.shape
    wb, a, wc = tables
    nt = s // ts
    ucol = 4 * A_WIDTH // B_WIDTH
    kernel = functools.partial(_s5_kernel, ts=ts)
    full = lambda shape: pl.BlockSpec(shape, lambda b, i: (0,) * len(shape))
    return pl.pallas_call(
        kernel,
        grid=(bt, nt),
        in_specs=[
            pl.BlockSpec((1, ts, B_WIDTH), lambda b, i: (b, i, ucol)),
            pl.BlockSpec((1, ts, B_WIDTH), lambda b, i: (b, nt - 1 - i, ucol)),
            full(wb.shape), full(a.shape), full(wc.shape),
        ],
        out_specs=[
            pl.BlockSpec((1, ts, B_WIDTH), lambda b, i: (b, i, 0)),
            pl.BlockSpec((1, ts, B_WIDTH), lambda b, i: (b, nt - 1 - i, 0)),
        ],
        out_shape=[jax.ShapeDtypeStruct((bt, s, B_WIDTH), F32)] * 2,
        scratch_shapes=[pltpu.VMEM((2, 1, 2 * B_STATES), F32), pltpu.VMEM((ts, 2 * B_STATES), F32)],
        compiler_params=_params("parallel", "arbitrary"),
    )(z, z, wb, a, wc)


def _gelu_tanh(x):
    return 0.5 * x * (1.0 + jnp.tanh(math.sqrt(2.0 / math.pi) * (x + 0.044715 * (x * x * x))))


def _silu(x):
    return x * jax.nn.sigmoid(x)


def _even_out_kernel(h_ref, ya_ref, yf_ref, yb_ref, u_ref, gb_ref, d_ref, gw_ref, gbias_ref, wa_ref, wb_ref,
                     gpost_ref, *rest):
    *xattn_refs, o_ref = rest
    y_s5 = _gelu_tanh(d_ref[...] * u_ref[0] + yf_ref[0] + yb_ref[0])
    y_glu = y_s5 * jax.nn.sigmoid(jnp.dot(y_s5.astype(BF16), gw_ref[...], preferred_element_type=F32)
                                  + gbias_ref[...])
    y_b = y_glu * _silu(gb_ref[0])
    y = (jnp.dot(ya_ref[0].astype(BF16), wa_ref[...], preferred_element_type=F32)
         + jnp.dot(y_b.astype(BF16), wb_ref[...], preferred_element_type=F32))
    o_ref[0] = _xattn_block(h_ref[0] + _rms(y, gpost_ref[...]), *xattn_refs)


def _even_out(h, y_a, y_f, y_b, z, d_skip, glu_w, glu_b, w_out, g_post, xattn, *, tm):
    bt, s, d = h.shape
    ucol = 4 * A_WIDTH // B_WIDTH
    row = lambda width, col=0: pl.BlockSpec((1, tm, width), lambda b, i: (b, i, col))
    full = lambda shape: pl.BlockSpec(shape, lambda b, i: (0,) * len(shape))
    x_specs, x_args = _xattn_operands(*xattn)
    return pl.pallas_call(
        _even_out_kernel,
        grid=(bt, s // tm),
        in_specs=[row(d), row(A_WIDTH), row(B_WIDTH), row(B_WIDTH), row(B_WIDTH, ucol), row(B_WIDTH, ucol + 1),
                  full((1, B_WIDTH)), full((B_WIDTH, B_WIDTH)), full((1, B_WIDTH)),
                  full((A_WIDTH, d)), full((B_WIDTH, d)), full((1, d))] + x_specs,
        out_specs=row(d),
        out_shape=jax.ShapeDtypeStruct((bt, s, d), F32),
        compiler_params=_params("parallel", "parallel"),
    )(h, y_a, y_f, y_b, z, z, d_skip.reshape(1, -1), glu_w, glu_b.reshape(1, -1),
      w_out[:A_WIDTH], w_out[A_WIDTH:], g_post.reshape(1, -1), *x_args)


def _odd_out_kernel(h_ref, of_ref, ob_ref, gc_ref, hf_ref, hb_ref, gd_ref, onorm_ref, wc_ref, wd_ref, gpost_ref,
                    *rest):
    *xattn_refs, o_ref = rest
    o_c = of_ref[0] + ob_ref[0]
    heads = [o_c[:, hd * HEAD_DIM:(hd + 1) * HEAD_DIM] for hd in range(C_HEADS)]
    o_c = jnp.concatenate([t * lax.rsqrt(jnp.mean(t * t, axis=-1, keepdims=True) + EPS) for t in heads], axis=-1)
    y_c = o_c * onorm_ref[...] * _silu(gc_ref[0])
    y_d = (hf_ref[0] + hb_ref[0]) * _silu(gd_ref[0])
    y = (jnp.dot(y_c.astype(BF16), wc_ref[...], preferred_element_type=F32)
         + jnp.dot(y_d.astype(BF16), wd_ref[...], preferred_element_type=F32))
    o_ref[0] = _xattn_block(h_ref[0] + _rms(y, gpost_ref[...]), *xattn_refs)


def _odd_out(h, o_f, o_b, h_f, h_b, z, onorm, w_out, g_post, xattn, *, tm):
    bt, s, d = h.shape
    gccol = 4 * C_WIDTH // C_WIDTH
    gdcol = (5 * C_WIDTH + D_WIDTH) // D_WIDTH
    row = lambda width, col=0: pl.BlockSpec((1, tm, width), lambda b, i: (b, i, col))
    full = lambda shape: pl.BlockSpec(shape, lambda b, i: (0,) * len(shape))
    x_specs, x_args = _xattn_operands(*xattn)
    return pl.pallas_call(
        _odd_out_kernel,
        grid=(bt, s // tm),
        in_specs=[row(d), row(C_WIDTH), row(C_WIDTH), row(C_WIDTH, gccol),
                  row(D_WIDTH), row(D_WIDTH), row(D_WIDTH, gdcol),
                  full((1, C_WIDTH)), full((C_WIDTH, d)), full((D_WIDTH, d)), full((1, d))] + x_specs,
        out_specs=row(d),
        out_shape=jax.ShapeDtypeStruct((bt, s, d), F32),
        compiler_params=_params("parallel", "parallel"),
    )(h, o_f, o_b, z, h_f, h_b, z, onorm.reshape(1, -1), w_out[:C_WIDTH], w_out[C_WIDTH:],
      g_post.reshape(1, -1), *x_args)


def _xattn_block(h, gpre_ref, wq_ref, kv_ref, wo_ref, gpost_ref):
    q = _dot(_rms(h, gpre_ref[...]), wq_ref[...]) * (HEAD_DIM ** -0.5)
    heads = []
    for hd in range(X_HEADS):
        cols = slice(hd * HEAD_DIM, (hd + 1) * HEAD_DIM)
        k = kv_ref[0, :, cols]
        v = kv_ref[0, :, X_WIDTH + hd * HEAD_DIM:X_WIDTH + (hd + 1) * HEAD_DIM]
        sc = _dot_t(q[:, cols], k)
        p = jnp.exp(sc - jnp.max(sc, axis=-1, keepdims=True))
        heads.append(_dot(p, v) / jnp.sum(p, axis=-1, keepdims=True))
    o = jnp.concatenate(heads, axis=-1)
    return h + _rms(_dot(o, wo_ref[...]), gpost_ref[...])


def _xattn_operands(kv, g_pre, wq, wo, g_post):
    d = wq.shape[0]
    full = lambda shape: pl.BlockSpec(shape, lambda b, i: (0,) * len(shape))
    specs = [full((1, d)), full((d, X_WIDTH)), pl.BlockSpec((1, N_MEM, 2 * X_WIDTH), lambda b, i: (b, 0, 0)),
             full((X_WIDTH, d)), full((1, d))]
    return specs, (g_pre.reshape(1, -1), wq, kv, wo, g_post.reshape(1, -1))


def _hgrn_kernel(q_ref, f_ref, v_ref, lb_ref, o_ref, state_ref, *, th, reverse):
    @pl.when(pl.program_id(2) == 0)
    def _():
        state_ref[...] = jnp.zeros(state_ref.shape, F32)

    lb = lb_ref[...]
    fx = f_ref[0]
    qf = _silu(q_ref[0])
    v = v_ref[0]
    logf = jnp.log(lb + (1.0 - lb) * jax.nn.sigmoid(fx))
    kf = (1.0 - lb) * jax.nn.sigmoid(-fx)

    t = lax.broadcasted_iota(jnp.int32, (th, th), 0)
    r = lax.broadcasted_iota(jnp.int32, (th, th), 1)
    same = (t // C_SUB) == (r // C_SUB)
    causal = same & ((r >= t) if reverse else (r <= t))
    cum = _permute_rows(causal.astype(BF16), logf)
    nblk = th // C_SUB
    half = C_SUB // 2
    mid_row = [j * C_SUB + (half if reverse else half - 1) for j in range(nblk)]
    tot_row = [j * C_SUB + (0 if reverse else C_SUB - 1) for j in range(nblk)]
    width = cum.shape[1]
    spread = lambda rows: jnp.concatenate(
        [jnp.broadcast_to(cum[i:i + 1], (C_SUB, width)) for i in rows], axis=0)
    mid = spread(mid_row)
    tot = spread(tot_row)
    q_mid = qf * jnp.exp(cum - mid)
    k_mid = kf * jnp.exp(mid - cum)
    q_in = qf * jnp.exp(cum)
    k_out = kf * jnp.exp(tot - cum)

    heads = [slice(n * HEAD_DIM, (n + 1) * HEAD_DIM) for n in range(width // HEAD_DIM)]
    att = [jnp.where(causal, _dot_t(q_mid[:, hd], k_mid[:, hd]), 0.0) for hd in heads]
    intra = [_dot(a, v[:, hd]) for a, hd in zip(att, heads)]
    state = [state_ref[n] for n in range(len(heads))]
    inter = [[None] * nblk for _ in heads]
    for j in (range(nblk - 1, -1, -1) if reverse else range(nblk)):
        rows = slice(j * C_SUB, (j + 1) * C_SUB)
        for n, hd in enumerate(heads):
            inter[n][j] = _dot_t(q_in[rows, hd], state[n])
            state[n] = (state[n] * jnp.exp(cum[tot_row[j]:tot_row[j] + 1, hd])
                        + _tdot(v[rows, hd], k_out[rows, hd]))
    for n in range(len(heads)):
        state_ref[n] = state[n]
    o_ref[0] = jnp.concatenate([a + jnp.concatenate(b, axis=0) for a, b in zip(intra, inter)], axis=1)


def _hgrn(z, lb, *, th, reverse):
    bt, s, _ = z.shape
    nt = s // th
    groups = C_HEADS // C_HEADS_PER_STEP
    width = C_HEADS_PER_STEP * HEAD_DIM
    fcol = 2 * groups if reverse else groups
    tile = (lambda i: nt - 1 - i) if reverse else (lambda i: i)
    col = lambda c0: pl.BlockSpec((1, th, width), lambda b, h, i: (b, tile(i), c0 + h))
    kernel = functools.partial(_hgrn_kernel, th=th, reverse=reverse)
    return pl.pallas_call(
        kernel,
        grid=(bt, groups, nt),
        in_specs=[col(0), col(fcol), col(3 * groups), pl.BlockSpec((1, width), lambda b, h, i: (0, h))],
        out_specs=col(0),
        out_shape=jax.ShapeDtypeStruct((bt, s, C_WIDTH), F32),
        scratch_shapes=[pltpu.VMEM((C_HEADS_PER_STEP, HEAD_DIM, HEAD_DIM), F32)],
        compiler_params=_params("parallel", "parallel", "arbitrary"),
    )(z, z, z, lb.reshape(1, C_WIDTH))


D_HALO = SUBLANES
LRU_COLS = 512


def _lru_scan(a_ref, bx_ref, carry_ref, *, tl, reverse):
    run = tl // SUBLANES
    sub = lax.broadcasted_iota(jnp.int32, (SUBLANES, LRU_COLS), 0)
    for c0 in range(0, D_WIDTH, LRU_COLS):
        cols = pl.ds(c0, LRU_COLS)

        def first(n, x):
            hcur, prod = x
            rows = _step_rows(n, run, reverse)
            a = a_ref[rows, cols]
            return a * hcur + bx_ref[rows, cols], a * prod

        def second(n, hcur):
            rows = _step_rows(n, run, reverse)
            hcur = a_ref[rows, cols] * hcur + bx_ref[rows, cols]
            bx_ref[rows, cols] = hcur
            return hcur

        zero = jnp.zeros((SUBLANES, LRU_COLS), F32)
        hcur, prod = lax.fori_loop(0, run, first, (zero, zero + 1.0), unroll=SCAN_UNROLL)
        c = carry_ref[0:1, cols]
        start = zero
        for s_ in (range(SUBLANES - 1, -1, -1) if reverse else range(SUBLANES)):
            start = jnp.where(sub == s_, c, start)
            c = prod[s_:s_ + 1] * c + hcur[s_:s_ + 1]
        carry_ref[0:1, cols] = c
        lax.fori_loop(0, run, second, start, unroll=SCAN_UNROLL)


def _lru_kernel(xf_ref, xfp_ref, xfn_ref, xb_ref, xbp_ref, xbn_ref, cw_ref, cb_ref, w_ref, bias_ref, sp_ref,
                hf_ref, hb_ref, carry_ref, a_ref, bx_ref, *, tl, nt):
    i = pl.program_id(1)
    to_runs = _run_major(tl, True)
    to_time = _run_major(tl, False)

    @pl.when(i == 0)
    def _():
        carry_ref[...] = jnp.zeros(carry_ref.shape, F32)

    dirs = ((xf_ref, xfp_ref, xfn_ref, hf_ref, i), (xb_ref, xbp_ref, xbn_ref, hb_ref, nt - 1 - i))
    for di, (x_ref, xp_ref, xn_ref, h_ref, tile) in enumerate(dirs):
        x = x_ref[0]
        prev = jnp.where(tile > 0, xp_ref[0], 0.0)
        nxt = jnp.where(tile < nt - 1, xn_ref[0], 0.0)
        ext = jnp.concatenate([prev, x, nxt], axis=0)
        xc = cb_ref[...]
        for tap in range(4):
            xc = xc + cw_ref[tap:tap + 1, :] * ext[D_HALO - 1 + tap:D_HALO - 1 + tap + tl]
        xc = _permute_rows(to_runs, xc)
        gates = []
        for n in range(D_BLOCKS):
            cols = slice(n * D_BLOCK_DIM, (n + 1) * D_BLOCK_DIM)
            gates.append(_dot(xc[:, cols], w_ref[di, n]))
        r = jax.nn.sigmoid(jnp.concatenate([g[:, :D_BLOCK_DIM] for g in gates], axis=-1) + bias_ref[di, 0:1])
        ig = jax.nn.sigmoid(jnp.concatenate([g[:, D_BLOCK_DIM:] for g in gates], axis=-1) + bias_ref[di, 1:2])
        log_a = -LRU_C * r * sp_ref[di:di + 1]
        a = jnp.exp(log_a)
        a_ref[...] = a
        bx_ref[...] = jnp.sqrt(-jnp.tanh(log_a) * (a * a + 1.0)) * (ig * xc)
        _lru_scan(a_ref, bx_ref, carry_ref.at[di], tl=tl, reverse=(di == 1))
        h_ref[0] = _permute_rows(to_time, bx_ref[...])


def _rglru(z, conv_w, conv_b, wr, br, wi, bi, lam, *, tl):
    bt, s, _ = z.shape
    nt = s // tl
    xcol = 5 * C_WIDTH // D_WIDTH
    hpt = tl // D_HALO
    nh = s // D_HALO
    w = jnp.concatenate([wr, wi], axis=-1).astype(BF16)
    bias = jnp.stack([br, bi], axis=1).astype(F32)
    sp = jax.nn.softplus(-lam.astype(F32))
    fwd = lambda i: i
    bwd = lambda i: nt - 1 - i
    specs = []
    for tile in (fwd, bwd):
        specs += [
            pl.BlockSpec((1, tl, D_WIDTH), lambda b, i, tile=tile: (b, tile(i), xcol)),
            pl.BlockSpec((1, D_HALO, D_WIDTH),
                         lambda b, i, tile=tile: (b, jnp.maximum(tile(i) * hpt - 1, 0), xcol)),
            pl.BlockSpec((1, D_HALO, D_WIDTH),
                         lambda b, i, tile=tile: (b, jnp.minimum((tile(i) + 1) * hpt, nh - 1), xcol)),
        ]
    full = lambda shape: pl.BlockSpec(shape, lambda b, i: (0,) * len(shape))
    kernel = functools.partial(_lru_kernel, tl=tl, nt=nt)
    return pl.pallas_call(
        kernel,
        grid=(bt, nt),
        in_specs=specs + [full((4, D_WIDTH)), full((1, D_WIDTH)), full(w.shape), full(bias.shape), full(sp.shape)],
        out_specs=[pl.BlockSpec((1, tl, D_WIDTH), lambda b, i: (b, i, 0)),
                   pl.BlockSpec((1, tl, D_WIDTH), lambda b, i: (b, nt - 1 - i, 0))],
        out_shape=[jax.ShapeDtypeStruct((bt, s, D_WIDTH), F32)] * 2,
        scratch_shapes=[pltpu.VMEM((2, 1, D_WIDTH), F32), pltpu.VMEM((tl, D_WIDTH), F32),
                        pltpu.VMEM((tl, D_WIDTH), F32)],
        compiler_params=_params("parallel", "arbitrary"),
    )(z, z, z, z, z, z, conv_w.astype(F32), conv_b.reshape(1, -1).astype(F32), w, bias, sp)


TM_PROJ = 1024
TN_PROJ = 1024
TQ_ATTN = 1024
TS_S5 = 256
TM_OUT = 256
TH_HGRN = 256
TL_LRU = 256


def _trunk(x, mem, p):
    bf = lambda t: t.astype(BF16)
    lb_all = jnp.cumsum(jax.nn.softmax(p['hgrn_lb_logits'].astype(F32), axis=0), axis=0)
    lb_all = lb_all - lb_all[0:1]
    slopes = jnp.asarray(2.0 ** (-8.0 * np.arange(1, A_HEADS + 1) / A_HEADS), dtype=F32)
    depth = p['norm_mix_pre'].shape[0]
    h = x
    for l in range(depth):
        kv = _norm_matmul(mem, p['x_mem_norm'][l], bf(p['x_wkv'][l]), tm=N_MEM, tn=2 * X_WIDTH)
        xattn = (kv, p['norm_x_pre'][l], bf(p['x_wq'][l]), bf(p['x_wo'][l]), p['norm_x_post'][l])
        if l % 2 == 0:
            e = l // 2
            z = _norm_matmul(h, p['norm_mix_pre'][l], bf(p['ev_w_in'][e]), tm=TM_PROJ, tn=TN_PROJ)
            y_a = _dilated_attention(z, slopes, tq=TQ_ATTN)
            tables = _s5_tables(p['s5_lam_re'][e], p['s5_lam_im'][e], p['s5_log_step'][e], p['s5_b_re'][e],
                                p['s5_b_im'][e], p['s5_c_re'][e], p['s5_c_im'][e], TS_S5 // SUBLANES)
            y_f, y_b = _s5(z, tables, ts=TS_S5)
            h = _even_out(h, y_a, y_f, y_b, z, p['s5_d'][e].astype(F32), bf(p['s5_glu_w'][e]),
                          p['s5_glu_b'][e].astype(F32), bf(p['ev_w_out'][e]), p['norm_mix_post'][l], xattn,
                          tm=TM_OUT)
        else:
            o = l // 2
            z = _norm_matmul(h, p['norm_mix_pre'][l], bf(p['od_w_in'][o]), tm=TM_PROJ, tn=TN_PROJ)
            o_f = _hgrn(z, lb_all[l], th=TH_HGRN, reverse=False)
            o_b = _hgrn(z, lb_all[l], th=TH_HGRN, reverse=True)
            h_f, h_b = _rglru(z, p['lru_conv_w'][o], p['lru_conv_b'][o], p['lru_wr'][o], p['lru_br'][o],
                              p['lru_wi'][o], p['lru_bi'][o], p['lru_lambda'][o], tl=TL_LRU)
            h = _odd_out(h, o_f, o_b, h_f, h_b, z, p['hgrn_onorm'][o].astype(F32), bf(p['od_w_out'][o]),
                         p['norm_mix_post'][l], xattn, tm=TM_OUT)
    return h


def kernel(x_prompt, x_sample, mem_prompt, mem_sample, norm_mix_pre, norm_mix_post, norm_x_pre, norm_x_post, ev_w_in, ev_w_out, s5_lam_re, s5_lam_im, s5_log_step, s5_b_re, s5_b_im, s5_c_re, s5_c_im, s5_d, s5_glu_w, s5_glu_b, od_w_in, od_w_out, hgrn_lb_logits, hgrn_onorm, lru_conv_w, lru_conv_b, lru_wr, lru_br, lru_wi, lru_bi, lru_lambda, x_wq, x_wkv, x_wo, x_mem_norm):
    params = dict(norm_mix_pre=norm_mix_pre, norm_mix_post=norm_mix_post, norm_x_pre=norm_x_pre,
                  norm_x_post=norm_x_post, ev_w_in=ev_w_in, ev_w_out=ev_w_out, s5_lam_re=s5_lam_re,
                  s5_lam_im=s5_lam_im, s5_log_step=s5_log_step, s5_b_re=s5_b_re, s5_b_im=s5_b_im,
                  s5_c_re=s5_c_re, s5_c_im=s5_c_im, s5_d=s5_d, s5_glu_w=s5_glu_w, s5_glu_b=s5_glu_b,
                  od_w_in=od_w_in, od_w_out=od_w_out, hgrn_lb_logits=hgrn_lb_logits, hgrn_onorm=hgrn_onorm,
                  lru_conv_w=lru_conv_w, lru_conv_b=lru_conv_b, lru_wr=lru_wr, lru_br=lru_br,
                  lru_wi=lru_wi, lru_bi=lru_bi, lru_lambda=lru_lambda, x_wq=x_wq, x_wkv=x_wkv,
                  x_wo=x_wo, x_mem_norm=x_mem_norm)
    return (_trunk(x_prompt, mem_prompt, params), _trunk(x_sample, mem_sample, params))
```
